```python
import math
import jax, jax.numpy as jnp
from jax import lax
import numpy as np

D_MODEL = 2048
BATCH = 2
SEQ = 16384
DEPTH = 1
DEC_BATCH = 4
DEC_SEQ = 4096
PAST_LEN = 128

GRID_W = 64
NA_HEADS = 8
NA_HD = 128
NA_WIN_ROWS = 8
NA_WIN_COLS = 16
DIFF_HEADS = 8
DIFF_DK = 64
DIFF_DV = 2 * DIFF_DK
Q_BLOCK = 128
ROPE_THETA = 10000.0
MEM_TOKENS = 256
MEM_HEADS = 4
MEM_HD = D_MODEL // MEM_HEADS
D_FF = ((8 * D_MODEL // 3 + 255) // 256) * 256
NA_W = NA_HEADS * NA_HD
DIFF_QK_W = DIFF_HEADS * 2 * DIFF_DK
DIFF_V_W = DIFF_HEADS * DIFF_DV
MIX_W = NA_W + DIFF_V_W
IN_W = 3 * NA_W + 2 * DIFF_QK_W + DIFF_V_W
RMS_EPS = 1e-6
SUBLN_EPS = 1e-5

kernel_name = "hymba_na_diffattn_encoder"


def rms_norm(x, g, eps=RMS_EPS):
    xf = x.astype(jnp.float32)
    y = xf * lax.rsqrt(jnp.mean(xf * xf, axis=-1, keepdims=True) + eps)
    return (y * g.astype(jnp.float32)).astype(x.dtype)


def rotary_tables(T):
    inv = 1.0 / (ROPE_THETA ** (jnp.arange(0, DIFF_DK, 2, dtype=jnp.float32) / DIFF_DK))
    ang = jnp.arange(T, dtype=jnp.float32)[:, None] * inv[None, :]
    ang = jnp.concatenate([ang, ang], axis=-1)
    return jnp.cos(ang), jnp.sin(ang)


def apply_rope(x, cos, sin):
    c = cos[None, :, None, None, :]
    s = sin[None, :, None, None, :]
    xf = x.astype(jnp.float32)
    x1, x2 = jnp.split(xf, 2, axis=-1)
    rot = jnp.concatenate([-x2, x1], axis=-1)
    return (xf * c + rot * s).astype(x.dtype)


def neighbourhood_attention(q, k, v, rpb):
    B, T = q.shape[0], q.shape[1]
    rows = T // GRID_W
    wr = min(NA_WIN_ROWS, rows)
    q4 = q.reshape(B, rows, GRID_W, NA_HEADS, NA_HD)
    k4 = k.reshape(B, rows, GRID_W, NA_HEADS, NA_HD)
    v4 = v.reshape(B, rows, GRID_W, NA_HEADS, NA_HD)
    col_start = np.clip(np.arange(GRID_W) - NA_WIN_COLS // 2, 0, GRID_W - NA_WIN_COLS)
    col_idx = col_start[:, None] + np.arange(NA_WIN_COLS)[None, :]
    dc_idx = col_idx - np.arange(GRID_W)[:, None] + (NA_WIN_COLS - 1)
    scale = NA_HD ** -0.5

    def row_block(r):
        rs = jnp.clip(r - wr // 2, 0, rows - wr)
        q_r = lax.dynamic_index_in_dim(q4, r, axis=1, keepdims=False)
        k_rows = lax.dynamic_slice_in_dim(k4, rs, wr, axis=1)
        v_rows = lax.dynamic_slice_in_dim(v4, rs, wr, axis=1)
        k_win = k_rows[:, :, col_idx]
        v_win = v_rows[:, :, col_idx]
        dr_idx = rs + jnp.arange(wr) - r + (NA_WIN_ROWS - 1)
        bias = rpb[:, dr_idx[None, :, None], dc_idx[:, None, :]]
        s = jnp.einsum('bchd,bicjhd->bhcij', q_r, k_win).astype(jnp.float32) * scale
        s = s + bias.astype(jnp.float32)[None]
        p = jax.nn.softmax(s.reshape(B, NA_HEADS, GRID_W, wr * NA_WIN_COLS), axis=-1)
        p = p.reshape(B, NA_HEADS, GRID_W, wr, NA_WIN_COLS).astype(v.dtype)
        return jnp.einsum('bhcij,bicjhd->bchd', p, v_win)

    out = lax.map(row_block, jnp.arange(rows, dtype=jnp.int32))
    return out.transpose(1, 0, 2, 3, 4).reshape(B, T, NA_W)


def differential_attention(q, k, v, lam, g_subln, lam_init):
    B, T = q.shape[0], q.shape[1]
    nblk = T // Q_BLOCK
    scale = DIFF_DK ** -0.5
    qb = q.reshape(B, nblk, Q_BLOCK, DIFF_HEADS, 2, DIFF_DK).transpose(1, 0, 2, 3, 4, 5)

    def q_block(qi):
        s = jnp.einsum('bqhsd,bkhsd->bhsqk', qi, k).astype(jnp.float32) * scale
        p = jax.nn.softmax(s, axis=-1)
        a = (p[:, :, 0] - lam * p[:, :, 1]).astype(v.dtype)
        return jnp.einsum('bhqk,bkhd->bqhd', a, v)

    o = lax.map(q_block, qb).transpose(1, 0, 2, 3, 4).reshape(B, T, DIFF_HEADS, DIFF_DV)
    o = rms_norm(o, g_subln, eps=SUBLN_EPS) * (1.0 - lam_init)
    return o.reshape(B, T, DIFF_V_W)


def memory_cross_attention(h, mem, g_x, g_mem, w_mq, w_mkv, w_mo):
    B, T = h.shape[0], h.shape[1]
    M = mem.shape[1]
    hn = rms_norm(h, g_x)
    mn = rms_norm(mem, g_mem)
    q = (hn @ w_mq).reshape(B, T, MEM_HEADS, MEM_HD)
    kv = mn @ w_mkv
    k = kv[..., :D_MODEL].reshape(B, M, MEM_HEADS, MEM_HD)
    v = kv[..., D_MODEL:].reshape(B, M, MEM_HEADS, MEM_HD)
    s = jnp.einsum('bqhd,bkhd->bhqk', q, k).astype(jnp.float32) * (MEM_HD ** -0.5)
    p = jax.nn.softmax(s, axis=-1).astype(v.dtype)
    o = jnp.einsum('bhqk,bkhd->bqhd', p, v).reshape(B, T, D_MODEL)
    return o @ w_mo


def swiglu_ffn(h, g_ffn, w_gate_up, w_down):
    hn = rms_norm(h, g_ffn)
    gu = hn @ w_gate_up
    gate, up = gu[..., :D_FF], gu[..., D_FF:]
    return (jax.nn.silu(gate) * up) @ w_down


def encoder_trunk(x, mem, g_mix, w_in, rpb, lam_q1, lam_k1, lam_q2, lam_k2, g_subln, w_out,
                  g_xattn, g_mem, w_mq, w_mkv, w_mo, g_ffn, w_gate_up, w_down, g_final):
    B, T = x.shape[0], x.shape[1]
    cos, sin = rotary_tables(T)
    splits = [NA_W, 2 * NA_W, 3 * NA_W, 3 * NA_W + DIFF_QK_W, 3 * NA_W + 2 * DIFF_QK_W]
    for l in range(DEPTH):
        lam_init = 0.8 - 0.6 * math.exp(-0.3 * l)
        xn = rms_norm(x, g_mix[l])
        proj = xn @ w_in[l]
        na_q, na_k, na_v, df_q, df_k, df_v = jnp.split(proj, splits, axis=-1)
        na_o = neighbourhood_attention(
            na_q.reshape(B, T, NA_HEADS, NA_HD),
            na_k.reshape(B, T, NA_HEADS, NA_HD),
            na_v.reshape(B, T, NA_HEADS, NA_HD), rpb[l])
        df_q = apply_rope(df_q.reshape(B, T, DIFF_HEADS, 2, DIFF_DK), cos, sin)
        df_k = apply_rope(df_k.reshape(B, T, DIFF_HEADS, 2, DIFF_DK), cos, sin)
        lam = (jnp.exp(jnp.sum(lam_q1[l].astype(jnp.float32) * lam_k1[l].astype(jnp.float32)))
               - jnp.exp(jnp.sum(lam_q2[l].astype(jnp.float32) * lam_k2[l].astype(jnp.float32)))
               + lam_init)
        df_o = differential_attention(df_q, df_k, df_v.reshape(B, T, DIFF_HEADS, DIFF_DV),
                                      lam, g_subln[l], lam_init)
        x = x + jnp.concatenate([na_o, df_o], axis=-1) @ w_out[l]
        x = x + memory_cross_attention(x, mem, g_xattn[l], g_mem[l], w_mq[l], w_mkv[l], w_mo[l])
        x = x + swiglu_ffn(x, g_ffn[l], w_gate_up[l], w_down[l])
    return rms_norm(x, g_final)


def setup_inputs(seed: int = 0) -> dict:
    key = jax.random.key(seed)
    ks = jax.random.split(key, 24)
    f32 = jnp.float32

    def w(k, shape, fan_in):
        return jax.random.normal(k, shape, f32) * (fan_in ** -0.5)

    def gain(k, shape):
        return 1.0 + 0.01 * jax.random.normal(k, shape, f32)

    return {
        "x_prompt": jax.random.normal(ks[0], (BATCH, SEQ, D_MODEL), f32),
        "x_sample": jax.random.normal(ks[1], (DEC_BATCH, DEC_SEQ, D_MODEL), f32),
        "mem_prompt": jax.random.normal(ks[2], (BATCH, MEM_TOKENS, D_MODEL), f32),
        "mem_sample": jax.random.normal(ks[3], (DEC_BATCH, MEM_TOKENS, D_MODEL), f32),
        "g_mix": gain(ks[4], (DEPTH, D_MODEL)),
        "w_in": w(ks[5], (DEPTH, D_MODEL, IN_W), D_MODEL),
        "rpb": 0.1 * jax.random.normal(ks[6], (DEPTH, NA_HEADS, 2 * NA_WIN_ROWS - 1, 2 * NA_WIN_COLS - 1), f32),
        "lam_q1": 0.1 * jax.random.normal(ks[7], (DEPTH, DIFF_DK), f32),
        "lam_k1": 0.1 * jax.random.normal(ks[8], (DEPTH, DIFF_DK), f32),
        "lam_q2": 0.1 * jax.random.normal(ks[9], (DEPTH, DIFF_DK), f32),
        "lam_k2": 0.1 * jax.random.normal(ks[10], (DEPTH, DIFF_DK), f32),
        "g_subln": gain(ks[11], (DEPTH, DIFF_DV)),
        "w_out": w(ks[12], (DEPTH, MIX_W, D_MODEL), MIX_W),
        "g_xattn": gain(ks[13], (DEPTH, D_MODEL)),
        "g_mem": gain(ks[14], (DEPTH, D_MODEL)),
        "w_mq": w(ks[15], (DEPTH, D_MODEL, D_MODEL), D_MODEL),
        "w_mkv": w(ks[16], (DEPTH, D_MODEL, 2 * D_MODEL), D_MODEL),
        "w_mo": w(ks[17], (DEPTH, D_MODEL, D_MODEL), D_MODEL),
        "g_ffn": gain(ks[18], (DEPTH, D_MODEL)),
        "w_gate_up": w(ks[19], (DEPTH, D_MODEL, 2 * D_FF), D_MODEL),
        "w_down": w(ks[20], (DEPTH, D_FF, D_MODEL), D_FF),
        "g_final": gain(ks[21], (D_MODEL,)),
    }


def reference(x_prompt, x_sample, mem_prompt, mem_sample, g_mix, w_in, rpb, lam_q1, lam_k1,
              lam_q2, lam_k2, g_subln, w_out, g_xattn, g_mem, w_mq, w_mkv, w_mo, g_ffn,
              w_gate_up, w_down, g_final):
    y_prompt = encoder_trunk(x_prompt, mem_prompt, g_mix, w_in, rpb, lam_q1, lam_k1, lam_q2, lam_k2,
                             g_subln, w_out, g_xattn, g_mem, w_mq, w_mkv, w_mo, g_ffn,
                             w_gate_up, w_down, g_final)
    y_sample = encoder_trunk(x_sample, mem_sample, g_mix, w_in, rpb, lam_q1, lam_k1, lam_q2, lam_k2,
                             g_subln, w_out, g_xattn, g_mem, w_mq, w_mkv, w_mo, g_ffn,
                             w_gate_up, w_down, g_final)
    return (y_prompt, y_sample)
```

```python
import functools
import math

import jax
import jax.numpy as jnp
import numpy as np
from jax import lax
from jax.experimental import pallas as pl
from jax.experimental.pallas import tpu as pltpu

GRID_W = 64
NA_HEADS = 8
NA_HD = 128
NA_WIN_ROWS = 8
NA_WIN_COLS = 16
DIFF_HEADS = 8
DIFF_DK = 64
DIFF_DV = 2 * DIFF_DK
ROPE_THETA = 10000.0
MEM_HEADS = 4
RMS_EPS = 1e-6
SUBLN_EPS = 1e-5
LAM_INIT = 0.8 - 0.6 * math.exp(-0.3 * 0)

LANES = 128
V7X_VMEM_BYTES = 64 * 1024 * 1024
VMEM_LIMIT_CAP = V7X_VMEM_BYTES - 6 * 1024 * 1024
INTERNAL_SCRATCH_ALLOWANCE = 12 * 1024 * 1024

NA_Q_ROWS = 4
NA_K_ROWS = NA_Q_ROWS + NA_WIN_ROWS
NA_TQ = NA_Q_ROWS * GRID_W
NA_TK = NA_K_ROWS * GRID_W
MASK_VALUE = -1e30

_NT = (((1,), (1,)), ((), ()))

_BF16 = jnp.bfloat16
_F32 = jnp.float32


def _vmem_limit(*block_bytes):
    return int(min(VMEM_LIMIT_CAP, 2 * sum(block_bytes) + INTERNAL_SCRATCH_ALLOWANCE))


def _nbytes(shape, dtype):
    return int(np.prod(shape)) * jnp.dtype(dtype).itemsize


def _norm_matmul_kernel(x_ref, g_ref, w_ref, *rest, rope_blocks, q_scale):
    if rope_blocks is None:
        o_ref, xn_ref = rest
    else:
        cos_ref, sin_up_ref, sin_dn_ref, o_ref, xn_ref = rest
    j = pl.program_id(1)

    @pl.when(j == 0)
    def _normalize():
        x = x_ref[...]
        inv = lax.rsqrt(jnp.mean(x * x, axis=-1, keepdims=True) + RMS_EPS)
        xn_ref[...] = ((x * inv) * g_ref[...]).astype(xn_ref.dtype)

    acc = jnp.dot(xn_ref[...], w_ref[...], preferred_element_type=_F32)
    if rope_blocks is None:
        o_ref[...] = acc.astype(o_ref.dtype)
        return

    lo, q_hi, hi = rope_blocks
    is_rope = jnp.logical_and(j >= lo, j < hi)

    @pl.when(jnp.logical_not(is_rope))
    def _plain():
        o_ref[...] = acc.astype(o_ref.dtype)

    @pl.when(is_rope)
    def _rotary():
        scale = jnp.where(j < q_hi, q_scale, 1.0).astype(_F32)
        cos, sin_up, sin_dn = cos_ref[...], sin_up_ref[...], sin_dn_ref[...]
        for c in range(acc.shape[1] // LANES):
            a = acc[:, c * LANES:(c + 1) * LANES]
            r = (a * cos + pltpu.roll(a, LANES - DIFF_DK // 2, 1) * sin_up
                 + pltpu.roll(a, DIFF_DK // 2, 1) * sin_dn)
            o_ref[:, c * LANES:(c + 1) * LANES] = (r * scale).astype(o_ref.dtype)


def _norm_matmul(x, g, w, *, tm, tn, seq_len=None, rope=None, name):
    m, d = x.shape
    n = w.shape[1]
    assert m % tm == 0 and n % tn == 0
    in_specs = [
        pl.BlockSpec((tm, d), lambda i, j: (i, 0)),
        pl.BlockSpec((1, d), lambda i, j: (0, 0)),
        pl.BlockSpec((d, tn), lambda i, j: (0, j)),
    ]
    args = [x, g.reshape(1, d), w]
    rope_blocks = None
    if rope is not None:
        tables, rope_blocks = rope
        assert seq_len % tm == 0
        tiles_per_seq = seq_len // tm
        for t in tables:
            in_specs.append(pl.BlockSpec((tm, LANES), lambda i, j: (i % tiles_per_seq, 0)))
            args.append(t)
    kernel = functools.partial(_norm_matmul_kernel, rope_blocks=rope_blocks,
                               q_scale=DIFF_DK ** -0.5)
    limit = _vmem_limit(_nbytes((tm, d), _F32), _nbytes((d, tn), _BF16), _nbytes((tm, tn), _BF16),
                        3 * _nbytes((tm, LANES), _F32), _nbytes((tm, d), _BF16) // 2,
                        _nbytes((tm, tn), _F32))
    return pl.pallas_call(
        kernel,
        grid=(m // tm, n // tn),
        in_specs=in_specs,
        out_specs=pl.BlockSpec((tm, tn), lambda i, j: (i, j)),
        out_shape=jax.ShapeDtypeStruct((m, n), _BF16),
        scratch_shapes=[pltpu.VMEM((tm, d), _BF16)],
        compiler_params=pltpu.CompilerParams(
            dimension_semantics=("parallel", "arbitrary"), vmem_limit_bytes=limit),
        name=name,
    )(*args)


def _rope_tables(seq_len):
    inv = 1.0 / (ROPE_THETA ** (jnp.arange(0, DIFF_DK, 2, dtype=_F32) / DIFF_DK))
    ang = jnp.arange(seq_len, dtype=_F32)[:, None] * inv[None, :]
    ang = jnp.tile(ang, (1, LANES // (DIFF_DK // 2)))
    cos, sin = jnp.cos(ang), jnp.sin(ang)
    first_half = jnp.asarray((np.arange(LANES) % DIFF_DK) < DIFF_DK // 2)[None, :]
    sin_up = jnp.where(first_half, -sin, 0.0)
    sin_dn = jnp.where(first_half, 0.0, sin)
    return cos, sin_up, sin_dn


def _na_window(variant, i):
    if variant == 0:
        return 0, NA_WIN_ROWS - 1 - i
    if variant == 1:
        return i, NA_WIN_ROWS // 2 - 1 - i
    return NA_K_ROWS - NA_WIN_ROWS, -1 - i


def _na_bias_kernel(rpb_ref, o_ref):
    h = pl.program_id(0)
    n_dc = 2 * NA_WIN_COLS - 1
    c = lax.broadcasted_iota(jnp.int32, (GRID_W, LANES), 0)
    lane = lax.broadcasted_iota(jnp.int32, (GRID_W, LANES), 1)
    kc = lane & (GRID_W - 1)
    cs = jnp.clip(c - NA_WIN_COLS // 2, 0, GRID_W - NA_WIN_COLS)
    col_ok = jnp.logical_and(kc >= cs, kc < cs + NA_WIN_COLS)
    dc = kc - c + (NA_WIN_COLS - 1)
    neg = jnp.full((GRID_W, LANES), MASK_VALUE, _F32)
    by_dr = []
    for d in range(2 * NA_WIN_ROWS - 1):
        u = neg
        for n in range(n_dc):
            u = jnp.where(dc == n, rpb_ref[h, d * n_dc + n], u)
        by_dr.append(jnp.where(col_ok, u, neg))
    low_half = lane < GRID_W
    for variant in range(3):
        for i in range(NA_Q_ROWS):
            first, dr0 = _na_window(variant, i)
            for t in range(NA_K_ROWS // 2):
                halves = []
                for jj in (2 * t, 2 * t + 1):
                    valid = first <= jj < first + NA_WIN_ROWS
                    halves.append(by_dr[dr0 + jj] if valid else neg)
                tile = (halves[0] if halves[0] is halves[1]
                        else jnp.where(low_half, halves[0], halves[1]))
                o_ref[variant, i * GRID_W:(i + 1) * GRID_W, t * LANES:(t + 1) * LANES] = tile


def _na_bias(rpb):
    n_tab = (2 * NA_WIN_ROWS - 1) * (2 * NA_WIN_COLS - 1)
    return pl.pallas_call(
        _na_bias_kernel,
        grid=(NA_HEADS,),
        in_specs=[pl.BlockSpec(memory_space=pltpu.SMEM)],
        out_specs=pl.BlockSpec((None, 3, NA_TQ, NA_TK), lambda h: (h, 0, 0, 0)),
        out_shape=jax.ShapeDtypeStruct((NA_HEADS, 3, NA_TQ, NA_TK), _F32),
        compiler_params=pltpu.CompilerParams(dimension_semantics=("parallel",)),
        name="na_bias",
    )(rpb.reshape(NA_HEADS, n_tab))


def _na_attn_kernel(q_ref, k_ref, v_ref, bias_ref, o_ref, *, rows):
    r = pl.program_id(2)
    last = pl.num_programs(2) - 1
    kr0 = jnp.clip(r * NA_Q_ROWS - NA_WIN_ROWS // 2, 0, rows - NA_K_ROWS)
    start = pl.multiple_of(kr0 * GRID_W, GRID_W)
    kb = k_ref[pl.ds(start, NA_TK), :]
    vb = v_ref[pl.ds(start, NA_TK), :]
    variant = jnp.where(r == 0, 0, jnp.where(r == last, 2, 1))
    s = lax.dot_general(q_ref[...], kb, _NT, preferred_element_type=_F32)
    s = s * (NA_HD ** -0.5) + bias_ref[variant]
    p = jnp.exp(s - jnp.max(s, axis=-1, keepdims=True))
    l = jnp.sum(p, axis=-1, keepdims=True)
    o = jnp.dot(p.astype(_BF16), vb, preferred_element_type=_F32) / l
    o_ref[...] = o.astype(o_ref.dtype)


def _na_attn(proj, bias, *, col_q, col_k, col_v):
    b, t, _ = proj.shape
    rows = t // GRID_W
    assert t % NA_TQ == 0 and rows >= NA_K_ROWS + NA_Q_ROWS
    limit = _vmem_limit(2 * _nbytes((t, NA_HD), _BF16), _nbytes((3, NA_TQ, NA_TK), _F32),
                        2 * _nbytes((NA_TQ, NA_HD), _BF16), 2 * _nbytes((NA_TQ, NA_TK), _F32))
    return pl.pallas_call(
        functools.partial(_na_attn_kernel, rows=rows),
        grid=(b, NA_HEADS, t // NA_TQ),
        in_specs=[
            pl.BlockSpec((None, NA_TQ, NA_HD), lambda bi, h, r: (bi, r, col_q + h)),
            pl.BlockSpec((None, t, NA_HD), lambda bi, h, r: (bi, 0, col_k + h)),
            pl.BlockSpec((None, t, NA_HD), lambda bi, h, r: (bi, 0, col_v + h)),
            pl.BlockSpec((None, 3, NA_TQ, NA_TK), lambda bi, h, r: (h, 0, 0, 0)),
        ],
        out_specs=pl.BlockSpec((None, NA_TQ, NA_HD), lambda bi, h, r: (bi, r, h)),
        out_shape=jax.ShapeDtypeStruct((b, t, NA_HEADS * NA_HD), _BF16),
        compiler_params=pltpu.CompilerParams(
            dimension_semantics=("parallel", "parallel", "arbitrary"), vmem_limit_bytes=limit),
        name="na_attn",
    )(proj, proj, proj, bias)


def _diff_attn_kernel(q_ref, k_ref, v_ref, lq1_ref, lk1_ref, lq2_ref, lk2_ref, g_ref, o_ref,
                      vt_ref, acc1_ref, acc2_ref, *, tk):
    n_chunks = vt_ref.shape[0]

    @pl.when(pl.program_id(2) == 0)
    def _transpose_values():
        def body(c, carry):
            off = pl.multiple_of(c * tk, tk)
            vt_ref[c] = v_ref[pl.ds(off, tk), :].astype(_F32).T.astype(vt_ref.dtype)
            return carry
        lax.fori_loop(0, n_chunks, body, 0)

    q = q_ref[...]
    tq = q.shape[0]
    lane = lax.broadcasted_iota(jnp.int32, q.shape, 1)
    zero = jnp.zeros_like(q)
    qa = jnp.where(lane < DIFF_DK, q, zero)
    qb = jnp.where(lane >= DIFF_DK, q, zero)
    acc1_ref[...] = jnp.zeros_like(acc1_ref)
    acc2_ref[...] = jnp.zeros_like(acc2_ref)

    def softmax_step(s, m, l, vt, acc_ref):
        m_new = jnp.maximum(m, jnp.max(s, axis=0, keepdims=True))
        alpha = jnp.exp(m - m_new)
        p = jnp.exp(s - m_new)
        l_new = alpha * l + jnp.sum(p, axis=0, keepdims=True)
        acc_ref[...] = alpha * acc_ref[...] + jnp.dot(vt, p.astype(_BF16),
                                                      preferred_element_type=_F32)
        return m_new, l_new

    def body(c, carry):
        m1, l1, m2, l2 = carry
        off = pl.multiple_of(c * tk, tk)
        kc = k_ref[pl.ds(off, tk), :]
        vt = vt_ref[c]
        s1 = lax.dot_general(kc, qa, _NT, preferred_element_type=_F32)
        m1, l1 = softmax_step(s1, m1, l1, vt, acc1_ref)
        s2 = lax.dot_general(kc, qb, _NT, preferred_element_type=_F32)
        m2, l2 = softmax_step(s2, m2, l2, vt, acc2_ref)
        return m1, l1, m2, l2

    m0 = jnp.full((1, tq), MASK_VALUE, _F32)
    l0 = jnp.zeros((1, tq), _F32)
    _, l1, _, l2 = lax.fori_loop(0, n_chunks, body, (m0, l0, m0, l0))

    lam = (jnp.exp(jnp.sum(lq1_ref[...] * lk1_ref[...], axis=-1, keepdims=True))
           - jnp.exp(jnp.sum(lq2_ref[...] * lk2_ref[...], axis=-1, keepdims=True)) + LAM_INIT)
    o = acc1_ref[...] / l1 - lam * (acc2_ref[...] / l2)
    inv = lax.rsqrt(jnp.mean(o * o, axis=0, keepdims=True) + SUBLN_EPS)
    o = ((o * inv) * g_ref[...]) * (1.0 - LAM_INIT)
    o_ref[...] = o.T.astype(o_ref.dtype)


def _diff_attn(proj, lam_q1, lam_k1, lam_q2, lam_k2, g_subln, *, col_q, col_k, col_v, tq, tk):
    b, t, _ = proj.shape
    assert t % tq == 0 and t % tk == 0
    lam_spec = pl.BlockSpec((1, DIFF_DK), lambda bi, h, i: (0, 0))
    limit = _vmem_limit(2 * _nbytes((t, LANES), _BF16), 2 * _nbytes((tq, LANES), _BF16),
                        _nbytes((t, LANES), _BF16) // 2, _nbytes((DIFF_DV, tq), _F32),
                        2 * _nbytes((tk, tq), _F32))
    return pl.pallas_call(
        functools.partial(_diff_attn_kernel, tk=tk),
        grid=(b, DIFF_HEADS, t // tq),
        in_specs=[
            pl.BlockSpec((None, tq, LANES), lambda bi, h, i: (bi, i, col_q + h)),
            pl.BlockSpec((None, t, LANES), lambda bi, h, i: (bi, 0, col_k + h)),
            pl.BlockSpec((None, t, LANES), lambda bi, h, i: (bi, 0, col_v + h)),
            lam_spec, lam_spec, lam_spec, lam_spec,
            pl.BlockSpec((DIFF_DV, 1), lambda bi, h, i: (0, 0)),
        ],
        out_specs=pl.BlockSpec((None, tq, DIFF_DV), lambda bi, h, i: (bi, i, h)),
        out_shape=jax.ShapeDtypeStruct((b, t, DIFF_HEADS * DIFF_DV), _BF16),
        scratch_shapes=[
            pltpu.VMEM((t // tk, DIFF_DV, tk), _BF16),
            pltpu.VMEM((DIFF_DV, tq), _F32),
            pltpu.VMEM((DIFF_DV, tq), _F32),
        ],
        compiler_params=pltpu.CompilerParams(
            dimension_semantics=("parallel", "parallel", "arbitrary"), vmem_limit_bytes=limit),
        name="diff_attn",
    )(proj, proj, proj, lam_q1.reshape(1, DIFF_DK), lam_k1.reshape(1, DIFF_DK),
      lam_q2.reshape(1, DIFF_DK), lam_k2.reshape(1, DIFF_DK), g_subln.reshape(DIFF_DV, 1))


def _out_proj_kernel(x_ref, na_ref, df_ref, w_ref, o_ref):
    n_na = na_ref.shape[1]
    acc = jnp.dot(na_ref[...], w_ref[:n_na, :], preferred_element_type=_F32)
    acc += jnp.dot(df_ref[...], w_ref[n_na:, :], preferred_element_type=_F32)
    o_ref[...] = x_ref[...] + acc


def _out_proj(x, na_o, df_o, w_out, *, tm, tn):
    m, d = x.shape
    k_na, k_df = na_o.shape[1], df_o.shape[1]
    assert m % tm == 0 and d % tn == 0 and w_out.shape == (k_na + k_df, d)
    limit = _vmem_limit(2 * _nbytes((tm, tn), _F32), _nbytes((tm, k_na + k_df), _BF16),
                        _nbytes((k_na + k_df, tn), _BF16))
    return pl.pallas_call(
        _out_proj_kernel,
        grid=(m // tm, d // tn),
        in_specs=[
            pl.BlockSpec((tm, tn), lambda i, j: (i, j)),
            pl.BlockSpec((tm, k_na), lambda i, j: (i, 0)),
            pl.BlockSpec((tm, k_df), lambda i, j: (i, 0)),
            pl.BlockSpec((k_na + k_df, tn), lambda i, j: (0, j)),
        ],
        out_specs=pl.BlockSpec((tm, tn), lambda i, j: (i, j)),
        out_shape=jax.ShapeDtypeStruct((m, d), _F32),
        compiler_params=pltpu.CompilerParams(
            dimension_semantics=("parallel", "parallel"), vmem_limit_bytes=limit),
        name="out_proj",
    )(x, na_o, df_o, w_out)


def _mem_attn_kernel(q_ref, kv_ref, h_ref, w_ref, o_ref, ctx_ref):
    d = q_ref.shape[1]
    hd = d // MEM_HEADS
    for h in range(MEM_HEADS):
        qh = q_ref[:, h * hd:(h + 1) * hd]
        kh = kv_ref[:, h * hd:(h + 1) * hd]
        vh = kv_ref[:, d + h * hd:d + (h + 1) * hd]
        s = lax.dot_general(qh, kh, _NT, preferred_element_type=_F32) * (hd ** -0.5)
        p = jnp.exp(s - jnp.max(s, axis=-1, keepdims=True))
        l = jnp.sum(p, axis=-1, keepdims=True)
        ctx = jnp.dot(p.astype(_BF16), vh, preferred_element_type=_F32) / l
        ctx_ref[:, h * hd:(h + 1) * hd] = ctx.astype(ctx_ref.dtype)
    o_ref[...] = h_ref[...] + jnp.dot(ctx_ref[...], w_ref[...], preferred_element_type=_F32)


def _mem_attn(q, kv, h1, w_mo, *, seq_len, mem_tokens, tm):
    m, d = h1.shape
    assert m % tm == 0 and seq_len % tm == 0
    tiles_per_seq = seq_len // tm
    limit = _vmem_limit(_nbytes((tm, d), _BF16), _nbytes((mem_tokens, 2 * d), _BF16),
                        2 * _nbytes((tm, d), _F32), _nbytes((d, d), _BF16),
                        _nbytes((tm, d), _BF16) // 2)
    return pl.pallas_call(
        _mem_attn_kernel,
        grid=(m // tm,),
        in_specs=[
            pl.BlockSpec((tm, d), lambda i: (i, 0)),
            pl.BlockSpec((mem_tokens, 2 * d), lambda i: (i // tiles_per_seq, 0)),
            pl.BlockSpec((tm, d), lambda i: (i, 0)),
            pl.BlockSpec((d, d), lambda i: (0, 0)),
        ],
        out_specs=pl.BlockSpec((tm, d), lambda i: (i, 0)),
        out_shape=jax.ShapeDtypeStruct((m, d), _F32),
        scratch_shapes=[pltpu.VMEM((tm, d), _BF16)],
        compiler_params=pltpu.CompilerParams(
            dimension_semantics=("parallel",), vmem_limit_bytes=limit),
        name="mem_attn",
    )(q, kv, h1, w_mo)


def _ffn_kernel(h_ref, g_ref, wg_ref, wu_ref, wd_ref, gf_ref, o_ref, hn_ref, acc_ref):
    f = pl.program_id(1)

    @pl.when(f == 0)
    def _normalize():
        x = h_ref[...]
        inv = lax.rsqrt(jnp.mean(x * x, axis=-1, keepdims=True) + RMS_EPS)
        hn_ref[...] = ((x * inv) * g_ref[...]).astype(hn_ref.dtype)
        acc_ref[...] = jnp.zeros_like(acc_ref)

    hn = hn_ref[...]
    gate = jnp.dot(hn, wg_ref[...], preferred_element_type=_F32)
    up = jnp.dot(hn, wu_ref[...], preferred_element_type=_F32)
    act = (gate * jax.nn.sigmoid(gate)) * up
    acc_ref[...] += jnp.dot(act.astype(_BF16), wd_ref[...], preferred_element_type=_F32)

    @pl.when(f == pl.num_programs(1) - 1)
    def _finish():
        y = h_ref[...] + acc_ref[...]
        inv = lax.rsqrt(jnp.mean(y * y, axis=-1, keepdims=True) + RMS_EPS)
        o_ref[...] = (y * inv) * gf_ref[...]


def _ffn(h2, g_ffn, w_gate_up, w_down, g_final, *, tm, tf):
    m, d = h2.shape
    d_ff = w_down.shape[0]
    assert m % tm == 0 and d_ff % tf == 0 and w_gate_up.shape == (d, 2 * d_ff)
    n_f = d_ff // tf
    limit = _vmem_limit(2 * _nbytes((tm, d), _F32), 3 * _nbytes((d, tf), _BF16),
                        _nbytes((tm, d), _BF16) // 2, _nbytes((tm, d), _F32) // 2,
                        2 * _nbytes((tm, tf), _F32))
    return pl.pallas_call(
        _ffn_kernel,
        grid=(m // tm, n_f),
        in_specs=[
            pl.BlockSpec((tm, d), lambda i, f: (i, 0)),
            pl.BlockSpec((1, d), lambda i, f: (0, 0)),
            pl.BlockSpec((d, tf), lambda i, f: (0, f)),
            pl.BlockSpec((d, tf), lambda i, f: (0, n_f + f)),
            pl.BlockSpec((tf, d), lambda i, f: (f, 0)),
            pl.BlockSpec((1, d), lambda i, f: (0, 0)),
        ],
        out_specs=pl.BlockSpec((tm, d), lambda i, f: (i, 0)),
        out_shape=jax.ShapeDtypeStruct((m, d), _F32),
        scratch_shapes=[pltpu.VMEM((tm, d), _BF16), pltpu.VMEM((tm, d), _F32)],
        compiler_params=pltpu.CompilerParams(
            dimension_semantics=("parallel", "arbitrary"), vmem_limit_bytes=limit),
        name="ffn",
    )(h2, g_ffn.reshape(1, d), w_gate_up, w_gate_up, w_down, g_final.reshape(1, d))


def _trunk(x, mem, p, na_bias):
    b, t, d = x.shape
    mem_tokens = mem.shape[1]
    m = b * t
    x2 = x.reshape(m, d)
    na_w = NA_HEADS * NA_HD
    qk_w = DIFF_HEADS * 2 * DIFF_DK
    tn_in = 512
    rope_lo = 3 * na_w // tn_in
    rope = (_rope_tables(t), (rope_lo, rope_lo + qk_w // tn_in, rope_lo + 2 * qk_w // tn_in))
    proj = _norm_matmul(x2, p["g_mix"], p["w_in"], tm=1024, tn=tn_in, seq_len=t, rope=rope,
                        name="in_proj")
    proj = proj.reshape(b, t, -1)
    blk = lambda col: col // LANES
    na_o = _na_attn(proj, na_bias, col_q=0, col_k=blk(na_w), col_v=blk(2 * na_w))
    df_o = _diff_attn(proj, p["lam_q1"], p["lam_k1"], p["lam_q2"], p["lam_k2"], p["g_subln"],
                      col_q=blk(3 * na_w), col_k=blk(3 * na_w + qk_w),
                      col_v=blk(3 * na_w + 2 * qk_w), tq=256, tk=512)
    h1 = _out_proj(x2, na_o.reshape(m, -1), df_o.reshape(m, -1), p["w_out"], tm=1024, tn=512)
    q_mem = _norm_matmul(h1, p["g_xattn"], p["w_mq"], tm=1024, tn=512, name="mem_q")
    kv_mem = _norm_matmul(mem.reshape(b * mem_tokens, d), p["g_mem"], p["w_mkv"],
                          tm=min(512, b * mem_tokens), tn=512, name="mem_kv")
    h2 = _mem_attn(q_mem, kv_mem, h1, p["w_mo"], seq_len=t, mem_tokens=mem_tokens, tm=512)
    y = _ffn(h2, p["g_ffn"], p["w_gate_up"], p["w_down"], p["g_final"], tm=512, tf=512)
    return y.reshape(b, t, d)


def kernel(x_prompt, x_sample, mem_prompt, mem_sample, g_mix, w_in, rpb, lam_q1, lam_k1,
           lam_q2, lam_k2, g_subln, w_out, g_xattn, g_mem, w_mq, w_mkv, w_mo, g_ffn,
           w_gate_up, w_down, g_final):
    assert w_in.shape[0] == 1, "single-layer trunk"
    p = dict(
        g_mix=g_mix[0], w_in=w_in[0].astype(_BF16), lam_q1=lam_q1[0], lam_k1=lam_k1[0],
        lam_q2=lam_q2[0], lam_k2=lam_k2[0], g_subln=g_subln[0], w_out=w_out[0].astype(_BF16),
        g_xattn=g_xattn[0], g_mem=g_mem[0], w_mq=w_mq[0].astype(_BF16),
        w_mkv=w_mkv[0].astype(_BF16), w_mo=w_mo[0].astype(_BF16), g_ffn=g_ffn[0],
        w_gate_up=w_gate_up[0].astype(_BF16), w_down=w_down[0].astype(_BF16), g_final=g_final,
    )
    na_bias = _na_bias(rpb[0])
    return (_trunk(x_prompt, mem_prompt, p, na_bias), _trunk(x_sample, mem_sample, p, na_bias))
```

```python
import functools
import math

import jax
import jax.numpy as jnp
import numpy as np
from jax import lax
from jax.experimental import pallas as pl
from jax.experimental.pallas import tpu as pltpu

GRID_W = 64
NA_HEADS = 8
NA_HD = 128
NA_WIN_ROWS = 8
NA_WIN_COLS = 16
DIFF_HEADS = 8
DIFF_DK = 64
DIFF_DV = 2 * DIFF_DK
ROPE_THETA = 10000.0
MEM_HEADS = 4
RMS_EPS = 1e-6
SUBLN_EPS = 1e-5
LAM_INIT = 0.8 - 0.6 * math.exp(-0.3 * 0)
DIFF_Q_SCALE = DIFF_DK ** -0.5 * math.log2(math.e)
BF16_SUBLANES = 16
DIFF_ACC_ROWS = DIFF_DV + BF16_SUBLANES

LANES = 128
MXU_COLS = 256
V7X_VMEM_BYTES = 64 * 1024 * 1024
VMEM_LIMIT_CAP = V7X_VMEM_BYTES - 6 * 1024 * 1024
INTERNAL_SCRATCH_ALLOWANCE = 12 * 1024 * 1024

NA_Q_ROWS = 4
NA_K_ROWS = NA_Q_ROWS + NA_WIN_ROWS
NA_TQ = NA_Q_ROWS * GRID_W
NA_TK = NA_K_ROWS * GRID_W
NA_BLOCKS_PER_STEP = 2
MASK_VALUE = -1e30

_NT = (((1,), (1,)), ((), ()))

_BF16 = jnp.bfloat16
_F32 = jnp.float32


def _vmem_limit(*block_bytes):
    return int(min(VMEM_LIMIT_CAP, 2 * sum(block_bytes) + INTERNAL_SCRATCH_ALLOWANCE))


def _nbytes(shape, dtype):
    return int(np.prod(shape)) * jnp.dtype(dtype).itemsize


def _norm_matmul_kernel(x_ref, g_ref, w_ref, *rest, rope_blocks, q_scale):
    if rope_blocks is None:
        o_ref, xn_ref = rest
    else:
        cos_ref, sin_up_ref, sin_dn_ref, o_ref, xn_ref = rest
    j = pl.program_id(1)

    @pl.when(j == 0)
    def _normalize():
        x = x_ref[...]
        inv = lax.rsqrt(jnp.mean(x * x, axis=-1, keepdims=True) + RMS_EPS)
        xn_ref[...] = ((x * inv) * g_ref[...]).astype(xn_ref.dtype)

    def _plain():
        acc = jnp.dot(xn_ref[...], w_ref[...], preferred_element_type=_F32)
        o_ref[...] = acc.astype(o_ref.dtype)

    if rope_blocks is None:
        _plain()
        return

    lo, q_hi, hi = rope_blocks
    is_rope = jnp.logical_and(j >= lo, j < hi)
    pl.when(jnp.logical_not(is_rope))(_plain)

    @pl.when(is_rope)
    def _rotary():
        scale = jnp.where(j < q_hi, q_scale, 1.0).astype(_F32)
        cos, sin_up, sin_dn = cos_ref[...], sin_up_ref[...], sin_dn_ref[...]
        xn = xn_ref[...]
        for n0 in range(0, o_ref.shape[1], MXU_COLS):
            acc = jnp.dot(xn, w_ref[:, n0:n0 + MXU_COLS], preferred_element_type=_F32)
            for c0 in range(0, MXU_COLS, LANES):
                a = acc[:, c0:c0 + LANES]
                r = (a * cos + pltpu.roll(a, LANES - DIFF_DK // 2, 1) * sin_up
                     + pltpu.roll(a, DIFF_DK // 2, 1) * sin_dn)
                o_ref[:, n0 + c0:n0 + c0 + LANES] = (r * scale).astype(o_ref.dtype)


def _norm_matmul(x, g, w, *, tm, tn, seq_len=None, rope=None, name):
    m, d = x.shape
    n = w.shape[1]
    assert m % tm == 0 and n % tn == 0
    in_specs = [
        pl.BlockSpec((tm, d), lambda i, j: (i, 0)),
        pl.BlockSpec((1, d), lambda i, j: (0, 0)),
        pl.BlockSpec((d, tn), lambda i, j: (0, j)),
    ]
    args = [x, g.reshape(1, d), w]
    rope_blocks = None
    if rope is not None:
        tables, rope_blocks = rope
        assert seq_len % tm == 0
        tiles_per_seq = seq_len // tm
        for t in tables:
            in_specs.append(pl.BlockSpec((tm, LANES), lambda i, j: (i % tiles_per_seq, 0)))
            args.append(t)
    kernel = functools.partial(_norm_matmul_kernel, rope_blocks=rope_blocks,
                               q_scale=DIFF_Q_SCALE)
    limit = _vmem_limit(_nbytes((tm, d), _F32), _nbytes((d, tn), _BF16), _nbytes((tm, tn), _BF16),
                        3 * _nbytes((tm, LANES), _F32), _nbytes((tm, d), _BF16) // 2,
                        _nbytes((tm, tn), _F32))
    return pl.pallas_call(
        kernel,
        grid=(m // tm, n // tn),
        in_specs=in_specs,
        out_specs=pl.BlockSpec((tm, tn), lambda i, j: (i, j)),
        out_shape=jax.ShapeDtypeStruct((m, n), _BF16),
        scratch_shapes=[pltpu.VMEM((tm, d), _BF16)],
        compiler_params=pltpu.CompilerParams(
            dimension_semantics=("parallel", "arbitrary"), vmem_limit_bytes=limit),
        name=name,
    )(*args)


def _rope_tables(seq_len):
    inv = 1.0 / (ROPE_THETA ** (jnp.arange(0, DIFF_DK, 2, dtype=_F32) / DIFF_DK))
    ang = jnp.arange(seq_len, dtype=_F32)[:, None] * inv[None, :]
    ang = jnp.tile(ang, (1, LANES // (DIFF_DK // 2)))
    cos, sin = jnp.cos(ang), jnp.sin(ang)
    first_half = jnp.asarray((np.arange(LANES) % DIFF_DK) < DIFF_DK // 2)[None, :]
    sin_up = jnp.where(first_half, -sin, 0.0)
    sin_dn = jnp.where(first_half, 0.0, sin)
    return cos, sin_up, sin_dn


def _na_window(variant, i):
    if variant == 0:
        return 0, NA_WIN_ROWS - 1 - i
    if variant == 1:
        return i, NA_WIN_ROWS // 2 - 1 - i
    return NA_K_ROWS - NA_WIN_ROWS, -1 - i


def _na_bias_kernel(rpb_ref, o_ref):
    h = pl.program_id(0)
    n_dc = 2 * NA_WIN_COLS - 1
    c = lax.broadcasted_iota(jnp.int32, (GRID_W, LANES), 0)
    lane = lax.broadcasted_iota(jnp.int32, (GRID_W, LANES), 1)
    kc = lane & (GRID_W - 1)
    cs = jnp.clip(c - NA_WIN_COLS // 2, 0, GRID_W - NA_WIN_COLS)
    col_ok = jnp.logical_and(kc >= cs, kc < cs + NA_WIN_COLS)
    dc = kc - c + (NA_WIN_COLS - 1)
    neg = jnp.full((GRID_W, LANES), MASK_VALUE, _F32)
    by_dr = []
    for d in range(2 * NA_WIN_ROWS - 1):
        u = neg
        for n in range(n_dc):
            u = jnp.where(dc == n, rpb_ref[h, d * n_dc + n], u)
        by_dr.append(jnp.where(col_ok, u, neg))
    low_half = lane < GRID_W
    for variant in range(3):
        for i in range(NA_Q_ROWS):
            first, dr0 = _na_window(variant, i)
            for t in range(NA_K_ROWS // 2):
                halves = []
                for jj in (2 * t, 2 * t + 1):
                    valid = first <= jj < first + NA_WIN_ROWS
                    halves.append(by_dr[dr0 + jj] if valid else neg)
                tile = (halves[0] if halves[0] is halves[1]
                        else jnp.where(low_half, halves[0], halves[1]))
                o_ref[variant, i * GRID_W:(i + 1) * GRID_W, t * LANES:(t + 1) * LANES] = tile


def _na_bias(rpb):
    n_tab = (2 * NA_WIN_ROWS - 1) * (2 * NA_WIN_COLS - 1)
    return pl.pallas_call(
        _na_bias_kernel,
        grid=(NA_HEADS,),
        in_specs=[pl.BlockSpec(memory_space=pltpu.SMEM)],
        out_specs=pl.BlockSpec((None, 3, NA_TQ, NA_TK), lambda h: (h, 0, 0, 0)),
        out_shape=jax.ShapeDtypeStruct((NA_HEADS, 3, NA_TQ, NA_TK), _F32),
        compiler_params=pltpu.CompilerParams(dimension_semantics=("parallel",)),
        name="na_bias",
    )(rpb.reshape(NA_HEADS, n_tab))


def _na_attn_kernel(q_ref, k_ref, v_ref, bias_ref, o_ref, *, rows):
    last = pl.num_programs(2) * NA_BLOCKS_PER_STEP - 1
    for u in range(NA_BLOCKS_PER_STEP):
        r = pl.program_id(2) * NA_BLOCKS_PER_STEP + u
        kr0 = jnp.clip(r * NA_Q_ROWS - NA_WIN_ROWS // 2, 0, rows - NA_K_ROWS)
        start = pl.multiple_of(kr0 * GRID_W, GRID_W)
        kb = k_ref[pl.ds(start, NA_TK), :]
        vb = v_ref[pl.ds(start, NA_TK), :]
        variant = jnp.where(r == 0, 0, jnp.where(r == last, 2, 1))
        q = q_ref[u * NA_TQ:(u + 1) * NA_TQ, :]
        s = lax.dot_general(q, kb, _NT, preferred_element_type=_F32)
        s = s * (NA_HD ** -0.5) + bias_ref[variant]
        p = jnp.exp(s - jnp.max(s, axis=-1, keepdims=True))
        l = jnp.sum(p, axis=-1, keepdims=True)
        o = jnp.dot(p.astype(_BF16), vb, preferred_element_type=_F32) / l
        o_ref[u * NA_TQ:(u + 1) * NA_TQ, :] = o.astype(o_ref.dtype)


def _na_attn(proj, bias, *, col_q, col_k, col_v):
    b, t, _ = proj.shape
    rows = t // GRID_W
    tq = NA_BLOCKS_PER_STEP * NA_TQ
    assert t % tq == 0 and rows >= NA_K_ROWS + NA_Q_ROWS
    limit = _vmem_limit(2 * _nbytes((t, NA_HD), _BF16), _nbytes((3, NA_TQ, NA_TK), _F32),
                        2 * _nbytes((tq, NA_HD), _BF16),
                        2 * NA_BLOCKS_PER_STEP * _nbytes((NA_TQ, NA_TK), _F32))
    return pl.pallas_call(
        functools.partial(_na_attn_kernel, rows=rows),
        grid=(b, NA_HEADS, t // tq),
        in_specs=[
            pl.BlockSpec((None, tq, NA_HD), lambda bi, h, r: (bi, r, col_q + h)),
            pl.BlockSpec((None, t, NA_HD), lambda bi, h, r: (bi, 0, col_k + h)),
            pl.BlockSpec((None, t, NA_HD), lambda bi, h, r: (bi, 0, col_v + h)),
            pl.BlockSpec((None, 3, NA_TQ, NA_TK), lambda bi, h, r: (h, 0, 0, 0)),
        ],
        out_specs=pl.BlockSpec((None, tq, NA_HD), lambda bi, h, r: (bi, r, h)),
        out_shape=jax.ShapeDtypeStruct((b, t, NA_HEADS * NA_HD), _BF16),
        compiler_params=pltpu.CompilerParams(
            dimension_semantics=("parallel", "parallel", "arbitrary"), vmem_limit_bytes=limit),
        name="na_attn",
    )(proj, proj, proj, bias)


def _diff_attn_kernel(q_ref, k_ref, v_ref, lq1_ref, lk1_ref, lq2_ref, lk2_ref, g_ref, o_ref,
                      vt_ref, s_ref, acc1_ref, acc2_ref, *, tk):
    n_chunks = vt_ref.shape[0]

    @pl.when(pl.program_id(2) == 0)
    def _transpose_values():
        row = lax.broadcasted_iota(jnp.int32, (BF16_SUBLANES, tk), 0)
        ones_tile = jnp.where(row == 0, 1.0, 0.0).astype(vt_ref.dtype)

        def body(c, carry):
            off = pl.multiple_of(c * tk, tk)
            vt_ref[c, :DIFF_DV, :] = v_ref[pl.ds(off, tk), :].astype(_F32).T.astype(vt_ref.dtype)
            vt_ref[c, DIFF_DV:, :] = ones_tile
            return carry
        lax.fori_loop(0, n_chunks, body, 0)

    q = q_ref[...]
    tq = q.shape[0]
    lane = lax.broadcasted_iota(jnp.int32, q.shape, 1)
    zero = jnp.zeros_like(q)
    qa = jnp.where(lane < DIFF_DK, q, zero)
    qb = jnp.where(lane >= DIFF_DK, q, zero)
    acc_refs = (acc1_ref, acc2_ref)
    for acc_ref in acc_refs:
        acc_ref[...] = jnp.zeros_like(acc_ref)

    def scores(c, slot):
        off = pl.multiple_of(c * tk, tk)
        kc = k_ref[pl.ds(off, tk), :]
        chunk_max = []
        for half, qh in enumerate((qa, qb)):
            s = lax.dot_general(kc, qh, _NT, preferred_element_type=_F32)
            s_ref[slot, half] = s
            chunk_max.append(jnp.max(s, axis=0, keepdims=True))
        return tuple(chunk_max)

    def update(c, slot, chunk_max, m_old):
        vt = vt_ref[c]
        m_out = []
        for half, acc_ref in enumerate(acc_refs):
            m_new = jnp.maximum(m_old[half], chunk_max[half])
            alpha = jnp.exp2(m_old[half] - m_new)
            p = jnp.exp2(s_ref[slot, half] - m_new).astype(_BF16)
            acc_ref[...] = alpha * acc_ref[...] + jnp.dot(vt, p, preferred_element_type=_F32)
            m_out.append(m_new)
        return tuple(m_out)

    def body(j, carry):
        m, max0 = carry
        c = 2 * j
        max1 = scores(c + 1, 1)
        m = update(c, 0, max0, m)
        max0 = scores(c + 2, 0)
        m = update(c + 1, 1, max1, m)
        return m, max0

    m0 = jnp.full((1, tq), MASK_VALUE, _F32)
    m, max0 = lax.fori_loop(0, n_chunks // 2 - 1, body, ((m0, m0), scores(0, 0)))
    max1 = scores(n_chunks - 1, 1)
    m = update(n_chunks - 2, 0, max0, m)
    update(n_chunks - 1, 1, max1, m)

    lam = (jnp.exp(jnp.sum(lq1_ref[...] * lk1_ref[...], axis=-1, keepdims=True))
           - jnp.exp(jnp.sum(lq2_ref[...] * lk2_ref[...], axis=-1, keepdims=True)) + LAM_INIT)
    o1 = acc1_ref[:DIFF_DV, :] / acc1_ref[DIFF_DV:DIFF_DV + 1, :]
    o2 = acc2_ref[:DIFF_DV, :] / acc2_ref[DIFF_DV:DIFF_DV + 1, :]
    o = o1 - lam * o2
    inv = lax.rsqrt(jnp.mean(o * o, axis=0, keepdims=True) + SUBLN_EPS)
    o = ((o * inv) * g_ref[...]) * (1.0 - LAM_INIT)
    o_ref[...] = o.T.astype(o_ref.dtype)


def _diff_attn(proj, lam_q1, lam_k1, lam_q2, lam_k2, g_subln, *, col_q, col_k, col_v, tq, tk):
    b, t, _ = proj.shape
    assert t % tq == 0 and t % (2 * tk) == 0
    lam_spec = pl.BlockSpec((1, DIFF_DK), lambda bi, h, i: (0, 0))
    limit = _vmem_limit(2 * _nbytes((t, LANES), _BF16), 2 * _nbytes((tq, LANES), _BF16),
                        _nbytes((t, LANES), _BF16) // 2, _nbytes((DIFF_DV, tq), _F32),
                        2 * _nbytes((tk, tq), _F32))
    return pl.pallas_call(
        functools.partial(_diff_attn_kernel, tk=tk),
        grid=(b, DIFF_HEADS, t // tq),
        in_specs=[
            pl.BlockSpec((None, tq, LANES), lambda bi, h, i: (bi, i, col_q + h)),
            pl.BlockSpec((None, t, LANES), lambda bi, h, i: (bi, 0, col_k + h)),
            pl.BlockSpec((None, t, LANES), lambda bi, h, i: (bi, 0, col_v + h)),
            lam_spec, lam_spec, lam_spec, lam_spec,
            pl.BlockSpec((DIFF_DV, 1), lambda bi, h, i: (0, 0)),
        ],
        out_specs=pl.BlockSpec((None, tq, DIFF_DV), lambda bi, h, i: (bi, i, h)),
        out_shape=jax.ShapeDtypeStruct((b, t, DIFF_HEADS * DIFF_DV), _BF16),
        scratch_shapes=[
            pltpu.VMEM((t // tk, DIFF_ACC_ROWS, tk), _BF16),
            pltpu.VMEM((2, 2, tk, tq), _F32),
            pltpu.VMEM((DIFF_ACC_ROWS, tq), _F32),
            pltpu.VMEM((DIFF_ACC_ROWS, tq), _F32),
        ],
        compiler_params=pltpu.CompilerParams(
            dimension_semantics=("parallel", "parallel", "arbitrary"), vmem_limit_bytes=limit),
        name="diff_attn",
    )(proj, proj, proj, lam_q1.reshape(1, DIFF_DK), lam_k1.reshape(1, DIFF_DK),
      lam_q2.reshape(1, DIFF_DK), lam_k2.reshape(1, DIFF_DK), g_subln.reshape(DIFF_DV, 1))


def _out_proj_kernel(x_ref, na_ref, df_ref, w_ref, o_ref):
    n_na = na_ref.shape[1]
    acc = jnp.dot(na_ref[...], w_ref[:n_na, :], preferred_element_type=_F32)
    acc += jnp.dot(df_ref[...], w_ref[n_na:, :], preferred_element_type=_F32)
    o_ref[...] = x_ref[...] + acc


def _out_proj(x, na_o, df_o, w_out, *, tm, tn):
    m, d = x.shape
    k_na, k_df = na_o.shape[1], df_o.shape[1]
    assert m % tm == 0 and d % tn == 0 and w_out.shape == (k_na + k_df, d)
    limit = _vmem_limit(2 * _nbytes((tm, tn), _F32), _nbytes((tm, k_na + k_df), _BF16),
                        _nbytes((k_na + k_df, tn), _BF16))
    return pl.pallas_call(
        _out_proj_kernel,
        grid=(m // tm, d // tn),
        in_specs=[
            pl.BlockSpec((tm, tn), lambda i, j: (i, j)),
            pl.BlockSpec((tm, k_na), lambda i, j: (i, 0)),
            pl.BlockSpec((tm, k_df), lambda i, j: (i, 0)),
            pl.BlockSpec((k_na + k_df, tn), lambda i, j: (0, j)),
        ],
        out_specs=pl.BlockSpec((tm, tn), lambda i, j: (i, j)),
        out_shape=jax.ShapeDtypeStruct((m, d), _F32),
        compiler_params=pltpu.CompilerParams(
            dimension_semantics=("parallel", "parallel"), vmem_limit_bytes=limit),
        name="out_proj",
    )(x, na_o, df_o, w_out)


def _mem_attn_kernel(q_ref, kv_ref, h_ref, w_ref, o_ref, ctx_ref):
    d = q_ref.shape[1]
    hd = d // MEM_HEADS
    for h in range(MEM_HEADS):
        qh = q_ref[:, h * hd:(h + 1) * hd]
        kh = kv_ref[:, h * hd:(h + 1) * hd]
        vh = kv_ref[:, d + h * hd:d + (h + 1) * hd]
        s = lax.dot_general(qh, kh, _NT, preferred_element_type=_F32) * (hd ** -0.5)
        p = jnp.exp(s - jnp.max(s, axis=-1, keepdims=True))
        l = jnp.sum(p, axis=-1, keepdims=True)
        ctx = jnp.dot(p.astype(_BF16), vh, preferred_element_type=_F32) / l
        ctx_ref[:, h * hd:(h + 1) * hd] = ctx.astype(ctx_ref.dtype)
    o_ref[...] = h_ref[...] + jnp.dot(ctx_ref[...], w_ref[...], preferred_element_type=_F32)


def _mem_attn(q, kv, h1, w_mo, *, seq_len, mem_tokens, tm):
    m, d = h1.shape
    assert m % tm == 0 and seq_len % tm == 0
    tiles_per_seq = seq_len // tm
    limit = _vmem_limit(_nbytes((tm, d), _BF16), _nbytes((mem_tokens, 2 * d), _BF16),
                        2 * _nbytes((tm, d), _F32), _nbytes((d, d), _BF16),
                        _nbytes((tm, d), _BF16) // 2)
    return pl.pallas_call(
        _mem_attn_kernel,
        grid=(m // tm,),
        in_specs=[
            pl.BlockSpec((tm, d), lambda i: (i, 0)),
            pl.BlockSpec((mem_tokens, 2 * d), lambda i: (i // tiles_per_seq, 0)),
            pl.BlockSpec((tm, d), lambda i: (i, 0)),
            pl.BlockSpec((d, d), lambda i: (0, 0)),
        ],
        out_specs=pl.BlockSpec((tm, d), lambda i: (i, 0)),
        out_shape=jax.ShapeDtypeStruct((m, d), _F32),
        scratch_shapes=[pltpu.VMEM((tm, d), _BF16)],
        compiler_params=pltpu.CompilerParams(
            dimension_semantics=("parallel",), vmem_limit_bytes=limit),
        name="mem_attn",
    )(q, kv, h1, w_mo)


def _ffn_kernel(h_ref, g_ref, wg_ref, wu_ref, wd_ref, gf_ref, o_ref, hn_ref, acc_ref):
    f = pl.program_id(1)

    @pl.when(f == 0)
    def _normalize():
        x = h_ref[...]
        inv = lax.rsqrt(jnp.mean(x * x, axis=-1, keepdims=True) + RMS_EPS)
        hn_ref[...] = ((x * inv) * g_ref[...]).astype(hn_ref.dtype)
        acc_ref[...] = jnp.zeros_like(acc_ref)

    hn = hn_ref[...]
    gate = jnp.dot(hn, wg_ref[...], preferred_element_type=_F32)
    up = jnp.dot(hn, wu_ref[...], preferred_element_type=_F32)
    act = (gate * jax.nn.sigmoid(gate)) * up
    acc_ref[...] += jnp.dot(act.astype(_BF16), wd_ref[...], preferred_element_type=_F32)

    @pl.when(f == pl.num_programs(1) - 1)
    def _finish():
        y = h_ref[...] + acc_ref[...]
        inv = lax.rsqrt(jnp.mean(y * y, axis=-1, keepdims=True) + RMS_EPS)
        o_ref[...] = (y * inv) * gf_ref[...]


def _ffn(h2, g_ffn, w_gate_up, w_down, g_final, *, tm, tf):
    m, d = h2.shape
    d_ff = w_down.shape[0]
    assert m % tm == 0 and d_ff % tf == 0 and w_gate_up.shape == (d, 2 * d_ff)
    n_f = d_ff // tf
    limit = _vmem_limit(2 * _nbytes((tm, d), _F32), 3 * _nbytes((d, tf), _BF16),
                        _nbytes((tm, d), _BF16) // 2, _nbytes((tm, d), _F32) // 2,
                        2 * _nbytes((tm, tf), _F32))
    return pl.pallas_call(
        _ffn_kernel,
        grid=(m // tm, n_f),
        in_specs=[
            pl.BlockSpec((tm, d), lambda i, f: (i, 0)),
            pl.BlockSpec((1, d), lambda i, f: (0, 0)),
            pl.BlockSpec((d, tf), lambda i, f: (0, f)),
            pl.BlockSpec((d, tf), lambda i, f: (0, n_f + f)),
            pl.BlockSpec((tf, d), lambda i, f: (f, 0)),
            pl.BlockSpec((1, d), lambda i, f: (0, 0)),
        ],
        out_specs=pl.BlockSpec((tm, d), lambda i, f: (i, 0)),
        out_shape=jax.ShapeDtypeStruct((m, d), _F32),
        scratch_shapes=[pltpu.VMEM((tm, d), _BF16), pltpu.VMEM((tm, d), _F32)],
        compiler_params=pltpu.CompilerParams(
            dimension_semantics=("parallel", "arbitrary"), vmem_limit_bytes=limit),
        name="ffn",
    )(h2, g_ffn.reshape(1, d), w_gate_up, w_gate_up, w_down, g_final.reshape(1, d))


def _trunk(x, mem, p, na_bias):
    b, t, d = x.shape
    mem_tokens = mem.shape[1]
    m = b * t
    x2 = x.reshape(m, d)
    na_w = NA_HEADS * NA_HD
    qk_w = DIFF_HEADS * 2 * DIFF_DK
    tn_in = 512
    rope_lo = 3 * na_w // tn_in
    rope = (_rope_tables(t), (rope_lo, rope_lo + qk_w // tn_in, rope_lo + 2 * qk_w // tn_in))
    proj = _norm_matmul(x2, p["g_mix"], p["w_in"], tm=1024, tn=tn_in, seq_len=t, rope=rope,
                        name="in_proj")
    proj = proj.reshape(b, t, -1)
    blk = lambda col: col // LANES
    na_o = _na_attn(proj, na_bias, col_q=0, col_k=blk(na_w), col_v=blk(2 * na_w))
    df_o = _diff_attn(proj, p["lam_q1"], p["lam_k1"], p["lam_q2"], p["lam_k2"], p["g_subln"],
                      col_q=blk(3 * na_w), col_k=blk(3 * na_w + qk_w),
                      col_v=blk(3 * na_w + 2 * qk_w), tq=512, tk=512)
    h1 = _out_proj(x2, na_o.reshape(m, -1), df_o.reshape(m, -1), p["w_out"], tm=1024, tn=512)
    q_mem = _norm_matmul(h1, p["g_xattn"], p["w_mq"], tm=1024, tn=512, name="mem_q")
    kv_mem = _norm_matmul(mem.reshape(b * mem_tokens, d), p["g_mem"], p["w_mkv"],
                          tm=min(512, b * mem_tokens), tn=512, name="mem_kv")
    h2 = _mem_attn(q_mem, kv_mem, h1, p["w_mo"], seq_len=t, mem_tokens=mem_tokens, tm=512)
    y = _ffn(h2, p["g_ffn"], p["w_gate_up"], p["w_down"], p["g_final"], tm=512, tf=512)
    return y.reshape(b, t, d)


def kernel(x_prompt, x_sample, mem_prompt, mem_sample, g_mix, w_in, rpb, lam_q1, lam_k1,
           lam_q2, lam_k2, g_subln, w_out, g_xattn, g_mem, w_mq, w_mkv, w_mo, g_ffn,
           w_gate_up, w_down, g_final):
    assert w_in.shape[0] == 1, "single-layer trunk"
    p = dict(
        g_mix=g_mix[0], w_in=w_in[0].astype(_BF16), lam_q1=lam_q1[0], lam_k1=lam_k1[0],
        lam_q2=lam_q2[0], lam_k2=lam_k2[0], g_subln=g_subln[0], w_out=w_out[0].astype(_BF16),
        g_xattn=g_xattn[0], g_mem=g_mem[0], w_mq=w_mq[0].astype(_BF16),
        w_mkv=w_mkv[0].astype(_BF16), w_mo=w_mo[0].astype(_BF16), g_ffn=g_ffn[0],
        w_gate_up=w_gate_up[0].astype(_BF16), w_down=w_down[0].astype(_BF16), g_final=g_final,
    )
    na_bias = _na_bias(rpb[0])
    return (_trunk(x_prompt, mem_prompt, p, na_bias), _trunk(x_sample, mem_sample, p, na_bias))
```

```python
import functools
import math

import jax
import jax.numpy as jnp
import numpy as np
from jax import lax
from jax.experimental import pallas as pl
from jax.experimental.pallas import tpu as pltpu

GRID_W = 64
NA_HEADS = 8
NA_HD = 128
NA_WIN_ROWS = 8
NA_WIN_COLS = 16
DIFF_HEADS = 8
DIFF_DK = 64
DIFF_DV = 2 * DIFF_DK
ROPE_THETA = 10000.0
MEM_HEADS = 4
RMS_EPS = 1e-6
SUBLN_EPS = 1e-5
LAM_INIT = 0.8 - 0.6 * math.exp(-0.3 * 0)
DIFF_Q_SCALE = DIFF_DK ** -0.5 * math.log2(math.e)
BF16_SUBLANES = 16
DIFF_ACC_ROWS = DIFF_DV + BF16_SUBLANES
DIFF_PAIRS_PER_TRIP = 3

LANES = 128
MXU_COLS = 256
V7X_VMEM_BYTES = 64 * 1024 * 1024
VMEM_LIMIT_CAP = V7X_VMEM_BYTES - 6 * 1024 * 1024
INTERNAL_SCRATCH_ALLOWANCE = 12 * 1024 * 1024

NA_Q_ROWS = 4
NA_K_ROWS = NA_Q_ROWS + NA_WIN_ROWS
NA_TQ = NA_Q_ROWS * GRID_W
NA_TK = NA_K_ROWS * GRID_W
NA_BLOCKS_PER_STEP = 8
ROPE_ROW_GROUP = 256
MASK_VALUE = -1e30

_NT = (((1,), (1,)), ((), ()))

_BF16 = jnp.bfloat16
_F32 = jnp.float32


def _vmem_limit(*block_bytes):
    return int(min(VMEM_LIMIT_CAP, 2 * sum(block_bytes) + INTERNAL_SCRATCH_ALLOWANCE))


def _nbytes(shape, dtype):
    return int(np.prod(shape)) * jnp.dtype(dtype).itemsize


def _norm_matmul_kernel(x_ref, g_ref, w_ref, *rest, rope_blocks, q_scale):
    if rope_blocks is None:
        o_ref, xn_ref = rest
    else:
        cos_ref, sin_up_ref, sin_dn_ref, o_ref, xn_ref = rest
    j = pl.program_id(1)

    @pl.when(j == 0)
    def _normalize():
        x = x_ref[...]
        inv = lax.rsqrt(jnp.mean(x * x, axis=-1, keepdims=True) + RMS_EPS)
        xn_ref[...] = ((x * inv) * g_ref[...]).astype(xn_ref.dtype)

    def _plain():
        acc = jnp.dot(xn_ref[...], w_ref[...], preferred_element_type=_F32)
        o_ref[...] = acc.astype(o_ref.dtype)

    if rope_blocks is None:
        _plain()
        return

    lo, q_hi, hi = rope_blocks
    is_rope = jnp.logical_and(j >= lo, j < hi)
    pl.when(jnp.logical_not(is_rope))(_plain)

    @pl.when(is_rope)
    def _rotary():
        scale = jnp.where(j < q_hi, q_scale, 1.0).astype(_F32)
        tm, tn = o_ref.shape
        rows = min(tm, ROPE_ROW_GROUP)
        for r0 in range(0, tm, rows):
            rs = slice(r0, r0 + rows)
            cos, sin_up, sin_dn = cos_ref[rs, :], sin_up_ref[rs, :], sin_dn_ref[rs, :]
            xn = xn_ref[rs, :]
            for n0 in range(0, tn, MXU_COLS):
                acc = jnp.dot(xn, w_ref[:, n0:n0 + MXU_COLS], preferred_element_type=_F32)
                for c0 in range(0, MXU_COLS, LANES):
                    a = acc[:, c0:c0 + LANES]
                    r = (a * cos + pltpu.roll(a, LANES - DIFF_DK // 2, 1) * sin_up
                         + pltpu.roll(a, DIFF_DK // 2, 1) * sin_dn)
                    o_ref[rs, n0 + c0:n0 + c0 + LANES] = (r * scale).astype(o_ref.dtype)


def _norm_matmul(x, g, w, *, tm, tn, seq_len=None, rope=None, name):
    m, d = x.shape
    n = w.shape[1]
    assert m % tm == 0 and n % tn == 0
    in_specs = [
        pl.BlockSpec((tm, d), lambda i, j: (i, 0)),
        pl.BlockSpec((1, d), lambda i, j: (0, 0)),
        pl.BlockSpec((d, tn), lambda i, j: (0, j)),
    ]
    args = [x, g.reshape(1, d), w]
    rope_blocks = None
    if rope is not None:
        tables, rope_blocks = rope
        assert seq_len % tm == 0
        tiles_per_seq = seq_len // tm
        for t in tables:
            in_specs.append(pl.BlockSpec((tm, LANES), lambda i, j: (i % tiles_per_seq, 0)))
            args.append(t)
    kernel = functools.partial(_norm_matmul_kernel, rope_blocks=rope_blocks,
                               q_scale=DIFF_Q_SCALE)
    limit = _vmem_limit(_nbytes((tm, d), _F32), _nbytes((d, tn), _BF16), _nbytes((tm, tn), _BF16),
                        3 * _nbytes((tm, LANES), _F32), _nbytes((tm, d), _BF16) // 2,
                        _nbytes((tm, tn), _F32))
    return pl.pallas_call(
        kernel,
        grid=(m // tm, n // tn),
        in_specs=in_specs,
        out_specs=pl.BlockSpec((tm, tn), lambda i, j: (i, j)),
        out_shape=jax.ShapeDtypeStruct((m, n), _BF16),
        scratch_shapes=[pltpu.VMEM((tm, d), _BF16)],
        compiler_params=pltpu.CompilerParams(
            dimension_semantics=("parallel", "arbitrary"), vmem_limit_bytes=limit),
        name=name,
    )(*args)


def _rope_tables(seq_len):
    inv = 1.0 / (ROPE_THETA ** (jnp.arange(0, DIFF_DK, 2, dtype=_F32) / DIFF_DK))
    ang = jnp.arange(seq_len, dtype=_F32)[:, None] * inv[None, :]
    ang = jnp.tile(ang, (1, LANES // (DIFF_DK // 2)))
    cos, sin = jnp.cos(ang), jnp.sin(ang)
    first_half = jnp.asarray((np.arange(LANES) % DIFF_DK) < DIFF_DK // 2)[None, :]
    sin_up = jnp.where(first_half, -sin, 0.0)
    sin_dn = jnp.where(first_half, 0.0, sin)
    return cos, sin_up, sin_dn


def _na_window(variant, i):
    if variant == 0:
        return 0, NA_WIN_ROWS - 1 - i
    if variant == 1:
        return i, NA_WIN_ROWS // 2 - 1 - i
    return NA_K_ROWS - NA_WIN_ROWS, -1 - i


def _na_bias_kernel(rpb_ref, o_ref):
    h = pl.program_id(0)
    n_dc = 2 * NA_WIN_COLS - 1
    c = lax.broadcasted_iota(jnp.int32, (GRID_W, LANES), 0)
    lane = lax.broadcasted_iota(jnp.int32, (GRID_W, LANES), 1)
    kc = lane & (GRID_W - 1)
    cs = jnp.clip(c - NA_WIN_COLS // 2, 0, GRID_W - NA_WIN_COLS)
    col_ok = jnp.logical_and(kc >= cs, kc < cs + NA_WIN_COLS)
    dc = kc - c + (NA_WIN_COLS - 1)
    neg = jnp.full((GRID_W, LANES), MASK_VALUE, _F32)
    by_dr = []
    for d in range(2 * NA_WIN_ROWS - 1):
        u = neg
        for n in range(n_dc):
            u = jnp.where(dc == n, rpb_ref[h, d * n_dc + n], u)
        by_dr.append(jnp.where(col_ok, u, neg))
    low_half = lane < GRID_W
    for variant in range(3):
        for i in range(NA_Q_ROWS):
            first, dr0 = _na_window(variant, i)
            for t in range(NA_K_ROWS // 2):
                halves = []
                for jj in (2 * t, 2 * t + 1):
                    valid = first <= jj < first + NA_WIN_ROWS
                    halves.append(by_dr[dr0 + jj] if valid else neg)
                tile = (halves[0] if halves[0] is halves[1]
                        else jnp.where(low_half, halves[0], halves[1]))
                o_ref[variant, i * GRID_W:(i + 1) * GRID_W, t * LANES:(t + 1) * LANES] = tile


def _na_bias(rpb):
    n_tab = (2 * NA_WIN_ROWS - 1) * (2 * NA_WIN_COLS - 1)
    return pl.pallas_call(
        _na_bias_kernel,
        grid=(NA_HEADS,),
        in_specs=[pl.BlockSpec(memory_space=pltpu.SMEM)],
        out_specs=pl.BlockSpec((None, 3, NA_TQ, NA_TK), lambda h: (h, 0, 0, 0)),
        out_shape=jax.ShapeDtypeStruct((NA_HEADS, 3, NA_TQ, NA_TK), _F32),
        compiler_params=pltpu.CompilerParams(dimension_semantics=("parallel",)),
        name="na_bias",
    )(rpb.reshape(NA_HEADS, n_tab))


def _na_attn_kernel(q_ref, k_ref, v_ref, bias_ref, o_ref, *, rows):
    last = pl.num_programs(2) * NA_BLOCKS_PER_STEP - 1
    for u in range(NA_BLOCKS_PER_STEP):
        r = pl.program_id(2) * NA_BLOCKS_PER_STEP + u
        kr0 = jnp.clip(r * NA_Q_ROWS - NA_WIN_ROWS // 2, 0, rows - NA_K_ROWS)
        start = pl.multiple_of(kr0 * GRID_W, GRID_W)
        kb = k_ref[pl.ds(start, NA_TK), :]
        vb = v_ref[pl.ds(start, NA_TK), :]
        variant = jnp.where(r == 0, 0, jnp.where(r == last, 2, 1))
        q = q_ref[u * NA_TQ:(u + 1) * NA_TQ, :]
        s = lax.dot_general(q, kb, _NT, preferred_element_type=_F32)
        s = s * (NA_HD ** -0.5) + bias_ref[variant]
        p = jnp.exp(s - jnp.max(s, axis=-1, keepdims=True))
        l = jnp.sum(p, axis=-1, keepdims=True)
        o = jnp.dot(p.astype(_BF16), vb, preferred_element_type=_F32) / l
        o_ref[u * NA_TQ:(u + 1) * NA_TQ, :] = o.astype(o_ref.dtype)


def _na_attn(proj, bias, *, col_q, col_k, col_v):
    b, t, _ = proj.shape
    rows = t // GRID_W
    tq = NA_BLOCKS_PER_STEP * NA_TQ
    assert t % tq == 0 and rows >= NA_K_ROWS + NA_Q_ROWS
    limit = _vmem_limit(2 * _nbytes((t, NA_HD), _BF16), _nbytes((3, NA_TQ, NA_TK), _F32),
                        2 * _nbytes((tq, NA_HD), _BF16),
                        2 * NA_BLOCKS_PER_STEP * _nbytes((NA_TQ, NA_TK), _F32))
    return pl.pallas_call(
        functools.partial(_na_attn_kernel, rows=rows),
        grid=(b, NA_HEADS, t // tq),
        in_specs=[
            pl.BlockSpec((None, tq, NA_HD), lambda bi, h, r: (bi, r, col_q + h)),
            pl.BlockSpec((None, t, NA_HD), lambda bi, h, r: (bi, 0, col_k + h)),
            pl.BlockSpec((None, t, NA_HD), lambda bi, h, r: (bi, 0, col_v + h)),
            pl.BlockSpec((None, 3, NA_TQ, NA_TK), lambda bi, h, r: (h, 0, 0, 0)),
        ],
        out_specs=pl.BlockSpec((None, tq, NA_HD), lambda bi, h, r: (bi, r, h)),
        out_shape=jax.ShapeDtypeStruct((b, t, NA_HEADS * NA_HD), _BF16),
        compiler_params=pltpu.CompilerParams(
            dimension_semantics=("parallel", "parallel", "arbitrary"), vmem_limit_bytes=limit),
        name="na_attn",
    )(proj, proj, proj, bias)


def _diff_attn_kernel(q_ref, k_ref, v_ref, lq1_ref, lk1_ref, lq2_ref, lk2_ref, g_ref, o_ref,
                      vt_ref, s_ref, acc1_ref, acc2_ref, *, tk, pairs_per_trip):
    n_chunks = vt_ref.shape[0]

    @pl.when(pl.program_id(2) == 0)
    def _transpose_values():
        row = lax.broadcasted_iota(jnp.int32, (BF16_SUBLANES, tk), 0)
        ones_tile = jnp.where(row == 0, 1.0, 0.0).astype(vt_ref.dtype)

        def body(c, carry):
            off = pl.multiple_of(c * tk, tk)
            vt_ref[c, :DIFF_DV, :] = v_ref[pl.ds(off, tk), :].astype(_F32).T.astype(vt_ref.dtype)
            vt_ref[c, DIFF_DV:, :] = ones_tile
            return carry
        lax.fori_loop(0, n_chunks, body, 0)

    q = q_ref[...]
    tq = q.shape[0]
    lane = lax.broadcasted_iota(jnp.int32, q.shape, 1)
    zero = jnp.zeros_like(q)
    qa = jnp.where(lane < DIFF_DK, q, zero)
    qb = jnp.where(lane >= DIFF_DK, q, zero)
    acc_refs = (acc1_ref, acc2_ref)
    for acc_ref in acc_refs:
        acc_ref[...] = jnp.zeros_like(acc_ref)

    def scores(c, slot):
        off = pl.multiple_of(c * tk, tk)
        kc = k_ref[pl.ds(off, tk), :]
        chunk_max = []
        for half, qh in enumerate((qa, qb)):
            s = lax.dot_general(kc, qh, _NT, preferred_element_type=_F32)
            s_ref[slot, half] = s
            chunk_max.append(jnp.max(s, axis=0, keepdims=True))
        return tuple(chunk_max)

    def update(c, slot, chunk_max, m_old):
        vt = vt_ref[c]
        m_out = []
        for half, acc_ref in enumerate(acc_refs):
            m_new = jnp.maximum(m_old[half], chunk_max[half])
            alpha = jnp.exp2(m_old[half] - m_new)
            p = jnp.exp2(s_ref[slot, half] - m_new).astype(_BF16)
            acc_ref[...] = alpha * acc_ref[...] + jnp.dot(vt, p, preferred_element_type=_F32)
            m_out.append(m_new)
        return tuple(m_out)

    def body(j, carry):
        m, max0 = carry
        for u in range(pairs_per_trip):
            c = 2 * (j * pairs_per_trip + u)
            max1 = scores(c + 1, 1)
            m = update(c, 0, max0, m)
            max0 = scores(c + 2, 0)
            m = update(c + 1, 1, max1, m)
        return m, max0

    m0 = jnp.full((1, tq), MASK_VALUE, _F32)
    n_trips = (n_chunks // 2 - 1) // pairs_per_trip
    m, max0 = lax.fori_loop(0, n_trips, body, ((m0, m0), scores(0, 0)))
    max1 = scores(n_chunks - 1, 1)
    m = update(n_chunks - 2, 0, max0, m)
    update(n_chunks - 1, 1, max1, m)

    lam = (jnp.exp(jnp.sum(lq1_ref[...] * lk1_ref[...], axis=-1, keepdims=True))
           - jnp.exp(jnp.sum(lq2_ref[...] * lk2_ref[...], axis=-1, keepdims=True)) + LAM_INIT)
    o1 = acc1_ref[:DIFF_DV, :] / acc1_ref[DIFF_DV:DIFF_DV + 1, :]
    o2 = acc2_ref[:DIFF_DV, :] / acc2_ref[DIFF_DV:DIFF_DV + 1, :]
    o = o1 - lam * o2
    inv = lax.rsqrt(jnp.mean(o * o, axis=0, keepdims=True) + SUBLN_EPS)
    o = ((o * inv) * g_ref[...]) * (1.0 - LAM_INIT)
    o_ref[...] = o.T.astype(o_ref.dtype)


def _diff_attn(proj, lam_q1, lam_k1, lam_q2, lam_k2, g_subln, *, col_q, col_k, col_v, tq, tk):
    b, t, _ = proj.shape
    assert t % tq == 0 and t % (2 * tk) == 0
    loop_pairs = t // (2 * tk) - 1
    pairs_per_trip = DIFF_PAIRS_PER_TRIP if loop_pairs % DIFF_PAIRS_PER_TRIP == 0 else 1
    lam_spec = pl.BlockSpec((1, DIFF_DK), lambda bi, h, i: (0, 0))
    limit = _vmem_limit(2 * _nbytes((t, LANES), _BF16), 2 * _nbytes((tq, LANES), _BF16),
                        _nbytes((t, LANES), _BF16) // 2, _nbytes((DIFF_DV, tq), _F32),
                        2 * _nbytes((tk, tq), _F32))
    return pl.pallas_call(
        functools.partial(_diff_attn_kernel, tk=tk, pairs_per_trip=pairs_per_trip),
        grid=(b, DIFF_HEADS, t // tq),
        in_specs=[
            pl.BlockSpec((None, tq, LANES), lambda bi, h, i: (bi, i, col_q + h)),
            pl.BlockSpec((None, t, LANES), lambda bi, h, i: (bi, 0, col_k + h)),
            pl.BlockSpec((None, t, LANES), lambda bi, h, i: (bi, 0, col_v + h)),
            lam_spec, lam_spec, lam_spec, lam_spec,
            pl.BlockSpec((DIFF_DV, 1), lambda bi, h, i: (0, 0)),
        ],
        out_specs=pl.BlockSpec((None, tq, DIFF_DV), lambda bi, h, i: (bi, i, h)),
        out_shape=jax.ShapeDtypeStruct((b, t, DIFF_HEADS * DIFF_DV), _BF16),
        scratch_shapes=[
            pltpu.VMEM((t // tk, DIFF_ACC_ROWS, tk), _BF16),
            pltpu.VMEM((2, 2, tk, tq), _F32),
            pltpu.VMEM((DIFF_ACC_ROWS, tq), _F32),
            pltpu.VMEM((DIFF_ACC_ROWS, tq), _F32),
        ],
        compiler_params=pltpu.CompilerParams(
            dimension_semantics=("parallel", "parallel", "arbitrary"), vmem_limit_bytes=limit),
        name="diff_attn",
    )(proj, proj, proj, lam_q1.reshape(1, DIFF_DK), lam_k1.reshape(1, DIFF_DK),
      lam_q2.reshape(1, DIFF_DK), lam_k2.reshape(1, DIFF_DK), g_subln.reshape(DIFF_DV, 1))


def _out_proj_kernel(x_ref, na_ref, df_ref, w_ref, o_ref):
    n_na = na_ref.shape[1]
    acc = jnp.dot(na_ref[...], w_ref[:n_na, :], preferred_element_type=_F32)
    acc += jnp.dot(df_ref[...], w_ref[n_na:, :], preferred_element_type=_F32)
    o_ref[...] = x_ref[...] + acc


def _out_proj(x, na_o, df_o, w_out, *, tm, tn):
    m, d = x.shape
    k_na, k_df = na_o.shape[1], df_o.shape[1]
    assert m % tm == 0 and d % tn == 0 and w_out.shape == (k_na + k_df, d)
    limit = _vmem_limit(2 * _nbytes((tm, tn), _F32), _nbytes((tm, k_na + k_df), _BF16),
                        _nbytes((k_na + k_df, tn), _BF16))
    return pl.pallas_call(
        _out_proj_kernel,
        grid=(m // tm, d // tn),
        in_specs=[
            pl.BlockSpec((tm, tn), lambda i, j: (i, j)),
            pl.BlockSpec((tm, k_na), lambda i, j: (i, 0)),
            pl.BlockSpec((tm, k_df), lambda i, j: (i, 0)),
            pl.BlockSpec((k_na + k_df, tn), lambda i, j: (0, j)),
        ],
        out_specs=pl.BlockSpec((tm, tn), lambda i, j: (i, j)),
        out_shape=jax.ShapeDtypeStruct((m, d), _F32),
        compiler_params=pltpu.CompilerParams(
            dimension_semantics=("parallel", "parallel"), vmem_limit_bytes=limit),
        name="out_proj",
    )(x, na_o, df_o, w_out)


def _mem_attn_kernel(q_ref, kv_ref, h_ref, w_ref, o_ref, ctx_ref):
    d = q_ref.shape[1]
    hd = d // MEM_HEADS
    for h in range(MEM_HEADS):
        qh = q_ref[:, h * hd:(h + 1) * hd]
        kh = kv_ref[:, h * hd:(h + 1) * hd]
        vh = kv_ref[:, d + h * hd:d + (h + 1) * hd]
        s = lax.dot_general(qh, kh, _NT, preferred_element_type=_F32) * (hd ** -0.5)
        p = jnp.exp(s - jnp.max(s, axis=-1, keepdims=True))
        l = jnp.sum(p, axis=-1, keepdims=True)
        ctx = jnp.dot(p.astype(_BF16), vh, preferred_element_type=_F32) / l
        ctx_ref[:, h * hd:(h + 1) * hd] = ctx.astype(ctx_ref.dtype)
    o_ref[...] = h_ref[...] + jnp.dot(ctx_ref[...], w_ref[...], preferred_element_type=_F32)


def _mem_attn(q, kv, h1, w_mo, *, seq_len, mem_tokens, tm):
    m, d = h1.shape
    assert m % tm == 0 and seq_len % tm == 0
    tiles_per_seq = seq_len // tm
    limit = _vmem_limit(_nbytes((tm, d), _BF16), _nbytes((mem_tokens, 2 * d), _BF16),
                        2 * _nbytes((tm, d), _F32), _nbytes((d, d), _BF16),
                        _nbytes((tm, d), _BF16) // 2)
    return pl.pallas_call(
        _mem_attn_kernel,
        grid=(m // tm,),
        in_specs=[
            pl.BlockSpec((tm, d), lambda i: (i, 0)),
            pl.BlockSpec((mem_tokens, 2 * d), lambda i: (i // tiles_per_seq, 0)),
            pl.BlockSpec((tm, d), lambda i: (i, 0)),
            pl.BlockSpec((d, d), lambda i: (0, 0)),
        ],
        out_specs=pl.BlockSpec((tm, d), lambda i: (i, 0)),
        out_shape=jax.ShapeDtypeStruct((m, d), _F32),
        scratch_shapes=[pltpu.VMEM((tm, d), _BF16)],
        compiler_params=pltpu.CompilerParams(
            dimension_semantics=("parallel",), vmem_limit_bytes=limit),
        name="mem_attn",
    )(q, kv, h1, w_mo)


def _ffn_kernel(h_ref, g_ref, wg_ref, wu_ref, wd_ref, gf_ref, o_ref, hn_ref, acc_ref):
    f = pl.program_id(1)

    @pl.when(f == 0)
    def _normalize():
        x = h_ref[...]
        inv = lax.rsqrt(jnp.mean(x * x, axis=-1, keepdims=True) + RMS_EPS)
        hn_ref[...] = ((x * inv) * g_ref[...]).astype(hn_ref.dtype)
        acc_ref[...] = jnp.zeros_like(acc_ref)

    hn = hn_ref[...]
    gate = jnp.dot(hn, wg_ref[...], preferred_element_type=_F32)
    up = jnp.dot(hn, wu_ref[...], preferred_element_type=_F32)
    act = (gate * jax.nn.sigmoid(gate)) * up
    acc_ref[...] += jnp.dot(act.astype(_BF16), wd_ref[...], preferred_element_type=_F32)

    @pl.when(f == pl.num_programs(1) - 1)
    def _finish():
        y = h_ref[...] + acc_ref[...]
        inv = lax.rsqrt(jnp.mean(y * y, axis=-1, keepdims=True) + RMS_EPS)
        o_ref[...] = (y * inv) * gf_ref[...]


def _ffn(h2, g_ffn, w_gate_up, w_down, g_final, *, tm, tf):
    m, d = h2.shape
    d_ff = w_down.shape[0]
    assert m % tm == 0 and d_ff % tf == 0 and w_gate_up.shape == (d, 2 * d_ff)
    n_f = d_ff // tf
    limit = _vmem_limit(2 * _nbytes((tm, d), _F32), 3 * _nbytes((d, tf), _BF16),
                        _nbytes((tm, d), _BF16) // 2, _nbytes((tm, d), _F32) // 2,
                        2 * _nbytes((tm, tf), _F32))
    return pl.pallas_call(
        _ffn_kernel,
        grid=(m // tm, n_f),
        in_specs=[
            pl.BlockSpec((tm, d), lambda i, f: (i, 0)),
            pl.BlockSpec((1, d), lambda i, f: (0, 0)),
            pl.BlockSpec((d, tf), lambda i, f: (0, f)),
            pl.BlockSpec((d, tf), lambda i, f: (0, n_f + f)),
            pl.BlockSpec((tf, d), lambda i, f: (f, 0)),
            pl.BlockSpec((1, d), lambda i, f: (0, 0)),
        ],
        out_specs=pl.BlockSpec((tm, d), lambda i, f: (i, 0)),
        out_shape=jax.ShapeDtypeStruct((m, d), _F32),
        scratch_shapes=[pltpu.VMEM((tm, d), _BF16), pltpu.VMEM((tm, d), _F32)],
        compiler_params=pltpu.CompilerParams(
            dimension_semantics=("parallel", "arbitrary"), vmem_limit_bytes=limit),
        name="ffn",
    )(h2, g_ffn.reshape(1, d), w_gate_up, w_gate_up, w_down, g_final.reshape(1, d))


def _trunk(x, mem, p, na_bias):
    b, t, d = x.shape
    mem_tokens = mem.shape[1]
    m = b * t
    x2 = x.reshape(m, d)
    na_w = NA_HEADS * NA_HD
    qk_w = DIFF_HEADS * 2 * DIFF_DK
    tn_in = 512
    rope_lo = 3 * na_w // tn_in
    rope = (_rope_tables(t), (rope_lo, rope_lo + qk_w // tn_in, rope_lo + 2 * qk_w // tn_in))
    proj = _norm_matmul(x2, p["g_mix"], p["w_in"], tm=1024, tn=tn_in, seq_len=t, rope=rope,
                        name="in_proj")
    proj = proj.reshape(b, t, -1)
    blk = lambda col: col // LANES
    na_o = _na_attn(proj, na_bias, col_q=0, col_k=blk(na_w), col_v=blk(2 * na_w))
    df_o = _diff_attn(proj, p["lam_q1"], p["lam_k1"], p["lam_q2"], p["lam_k2"], p["g_subln"],
                      col_q=blk(3 * na_w), col_k=blk(3 * na_w + qk_w),
                      col_v=blk(3 * na_w + 2 * qk_w), tq=512, tk=512)
    h1 = _out_proj(x2, na_o.reshape(m, -1), df_o.reshape(m, -1), p["w_out"], tm=1024, tn=512)
    q_mem = _norm_matmul(h1, p["g_xattn"], p["w_mq"], tm=1024, tn=512, name="mem_q")
    kv_mem = _norm_matmul(mem.reshape(b * mem_tokens, d), p["g_mem"], p["w_mkv"],
                          tm=min(512, b * mem_tokens), tn=512, name="mem_kv")
    h2 = _mem_attn(q_mem, kv_mem, h1, p["w_mo"], seq_len=t, mem_tokens=mem_tokens, tm=512)
    y = _ffn(h2, p["g_ffn"], p["w_gate_up"], p["w_down"], p["g_final"], tm=512, tf=512)
    return y.reshape(b, t, d)


def kernel(x_prompt, x_sample, mem_prompt, mem_sample, g_mix, w_in, rpb, lam_q1, lam_k1,
           lam_q2, lam_k2, g_subln, w_out, g_xattn, g_mem, w_mq, w_mkv, w_mo, g_ffn,
           w_gate_up, w_down, g_final):
    assert w_in.shape[0] == 1, "single-layer trunk"
    p = dict(
        g_mix=g_mix[0], w_in=w_in[0].astype(_BF16), lam_q1=lam_q1[0], lam_k1=lam_k1[0],
        lam_q2=lam_q2[0], lam_k2=lam_k2[0], g_subln=g_subln[0], w_out=w_out[0].astype(_BF16),
        g_xattn=g_xattn[0], g_mem=g_mem[0], w_mq=w_mq[0].astype(_BF16),
        w_mkv=w_mkv[0].astype(_BF16), w_mo=w_mo[0].astype(_BF16), g_ffn=g_ffn[0],
        w_gate_up=w_gate_up[0].astype(_BF16), w_down=w_down[0].astype(_BF16), g_final=g_final,
    )
    na_bias = _na_bias(rpb[0])
    return (_trunk(x_prompt, mem_prompt, p, na_bias), _trunk(x_sample, mem_sample, p, na_bias))
```

```python
import functools
import math

import jax
import jax.numpy as jnp
import numpy as np
from jax import lax
from jax.experimental import pallas as pl
from jax.experimental.pallas import tpu as pltpu

GRID_W = 64
NA_HEADS = 8
NA_HD = 128
NA_WIN_ROWS = 8
NA_WIN_COLS = 16
DIFF_HEADS = 8
DIFF_DK = 64
DIFF_DV = 2 * DIFF_DK
ROPE_THETA = 10000.0
MEM_HEADS = 4
RMS_EPS = 1e-6
SUBLN_EPS = 1e-5
LAM_INIT = 0.8 - 0.6 * math.exp(-0.3 * 0)
DIFF_Q_SCALE = DIFF_DK ** -0.5 * math.log2(math.e)
BF16_SUBLANES = 16
DIFF_ACC_ROWS = DIFF_DV + BF16_SUBLANES
DIFF_CHUNKS_PER_TRIP = 6
DIFF_MAX_LAG = 64.0

LANES = 128
MXU_COLS = 256
V7X_VMEM_BYTES = 64 * 1024 * 1024
VMEM_LIMIT_CAP = V7X_VMEM_BYTES - 6 * 1024 * 1024
INTERNAL_SCRATCH_ALLOWANCE = 12 * 1024 * 1024

NA_Q_ROWS = 4
NA_K_ROWS = NA_Q_ROWS + NA_WIN_ROWS
NA_TQ = NA_Q_ROWS * GRID_W
NA_TK = NA_K_ROWS * GRID_W
NA_BLOCKS_PER_STEP = 8
ROPE_ROW_GROUP = 256
MASK_VALUE = -1e30

_NT = (((1,), (1,)), ((), ()))

_BF16 = jnp.bfloat16
_F32 = jnp.float32


def _vmem_limit(*block_bytes):
    return int(min(VMEM_LIMIT_CAP, 2 * sum(block_bytes) + INTERNAL_SCRATCH_ALLOWANCE))


def _nbytes(shape, dtype):
    return int(np.prod(shape)) * jnp.dtype(dtype).itemsize


def _norm_matmul_kernel(x_ref, g_ref, w_ref, *rest, rope_blocks, q_scale):
    if rope_blocks is None:
        o_ref, xn_ref = rest
    else:
        cos_ref, sin_up_ref, sin_dn_ref, o_ref, xn_ref = rest
    j = pl.program_id(1)

    @pl.when(j == 0)
    def _normalize():
        x = x_ref[...]
        inv = lax.rsqrt(jnp.mean(x * x, axis=-1, keepdims=True) + RMS_EPS)
        xn_ref[...] = ((x * inv) * g_ref[...]).astype(xn_ref.dtype)

    def _plain():
        acc = jnp.dot(xn_ref[...], w_ref[...], preferred_element_type=_F32)
        o_ref[...] = acc.astype(o_ref.dtype)

    if rope_blocks is None:
        _plain()
        return

    lo, q_hi, hi = rope_blocks
    is_rope = jnp.logical_and(j >= lo, j < hi)
    pl.when(jnp.logical_not(is_rope))(_plain)

    @pl.when(is_rope)
    def _rotary():
        scale = jnp.where(j < q_hi, q_scale, 1.0).astype(_F32)
        tm, tn = o_ref.shape
        rows = min(tm, ROPE_ROW_GROUP)
        for r0 in range(0, tm, rows):
            rs = slice(r0, r0 + rows)
            cos, sin_up, sin_dn = cos_ref[rs, :], sin_up_ref[rs, :], sin_dn_ref[rs, :]
            xn = xn_ref[rs, :]
            for n0 in range(0, tn, MXU_COLS):
                acc = jnp.dot(xn, w_ref[:, n0:n0 + MXU_COLS], preferred_element_type=_F32)
                for c0 in range(0, MXU_COLS, LANES):
                    a = acc[:, c0:c0 + LANES]
                    r = (a * cos + pltpu.roll(a, LANES - DIFF_DK // 2, 1) * sin_up
                         + pltpu.roll(a, DIFF_DK // 2, 1) * sin_dn)
                    o_ref[rs, n0 + c0:n0 + c0 + LANES] = (r * scale).astype(o_ref.dtype)


def _norm_matmul(x, g, w, *, tm, tn, seq_len=None, rope=None, name):
    m, d = x.shape
    n = w.shape[1]
    assert m % tm == 0 and n % tn == 0
    in_specs = [
        pl.BlockSpec((tm, d), lambda i, j: (i, 0)),
        pl.BlockSpec((1, d), lambda i, j: (0, 0)),
        pl.BlockSpec((d, tn), lambda i, j: (0, j)),
    ]
    args = [x, g.reshape(1, d), w]
    rope_blocks = None
    if rope is not None:
        tables, rope_blocks = rope
        assert seq_len % tm == 0
        tiles_per_seq = seq_len // tm
        for t in tables:
            in_specs.append(pl.BlockSpec((tm, LANES), lambda i, j: (i % tiles_per_seq, 0)))
            args.append(t)
    kernel = functools.partial(_norm_matmul_kernel, rope_blocks=rope_blocks,
                               q_scale=DIFF_Q_SCALE)
    limit = _vmem_limit(_nbytes((tm, d), _F32), _nbytes((d, tn), _BF16), _nbytes((tm, tn), _BF16),
                        3 * _nbytes((tm, LANES), _F32), _nbytes((tm, d), _BF16) // 2,
                        _nbytes((tm, tn), _F32))
    return pl.pallas_call(
        kernel,
        grid=(m // tm, n // tn),
        in_specs=in_specs,
        out_specs=pl.BlockSpec((tm, tn), lambda i, j: (i, j)),
        out_shape=jax.ShapeDtypeStruct((m, n), _BF16),
        scratch_shapes=[pltpu.VMEM((tm, d), _BF16)],
        compiler_params=pltpu.CompilerParams(
            dimension_semantics=("parallel", "arbitrary"), vmem_limit_bytes=limit),
        name=name,
    )(*args)


def _rope_tables(seq_len):
    inv = 1.0 / (ROPE_THETA ** (jnp.arange(0, DIFF_DK, 2, dtype=_F32) / DIFF_DK))
    ang = jnp.arange(seq_len, dtype=_F32)[:, None] * inv[None, :]
    ang = jnp.tile(ang, (1, LANES // (DIFF_DK // 2)))
    cos, sin = jnp.cos(ang), jnp.sin(ang)
    first_half = jnp.asarray((np.arange(LANES) % DIFF_DK) < DIFF_DK // 2)[None, :]
    sin_up = jnp.where(first_half, -sin, 0.0)
    sin_dn = jnp.where(first_half, 0.0, sin)
    return cos, sin_up, sin_dn


def _na_window(variant, i):
    if variant == 0:
        return 0, NA_WIN_ROWS - 1 - i
    if variant == 1:
        return i, NA_WIN_ROWS // 2 - 1 - i
    return NA_K_ROWS - NA_WIN_ROWS, -1 - i


def _na_bias_kernel(rpb_ref, o_ref):
    h = pl.program_id(0)
    n_dc = 2 * NA_WIN_COLS - 1
    c = lax.broadcasted_iota(jnp.int32, (GRID_W, LANES), 0)
    lane = lax.broadcasted_iota(jnp.int32, (GRID_W, LANES), 1)
    kc = lane & (GRID_W - 1)
    cs = jnp.clip(c - NA_WIN_COLS // 2, 0, GRID_W - NA_WIN_COLS)
    col_ok = jnp.logical_and(kc >= cs, kc < cs + NA_WIN_COLS)
    dc = kc - c + (NA_WIN_COLS - 1)
    neg = jnp.full((GRID_W, LANES), MASK_VALUE, _F32)
    by_dr = []
    for d in range(2 * NA_WIN_ROWS - 1):
        u = neg
        for n in range(n_dc):
            u = jnp.where(dc == n, rpb_ref[h, d * n_dc + n], u)
        by_dr.append(jnp.where(col_ok, u, neg))
    low_half = lane < GRID_W
    for variant in range(3):
        for i in range(NA_Q_ROWS):
            first, dr0 = _na_window(variant, i)
            for t in range(NA_K_ROWS // 2):
                halves = []
                for jj in (2 * t, 2 * t + 1):
                    valid = first <= jj < first + NA_WIN_ROWS
                    halves.append(by_dr[dr0 + jj] if valid else neg)
                tile = (halves[0] if halves[0] is halves[1]
                        else jnp.where(low_half, halves[0], halves[1]))
                o_ref[variant, i * GRID_W:(i + 1) * GRID_W, t * LANES:(t + 1) * LANES] = tile


def _na_bias(rpb):
    n_tab = (2 * NA_WIN_ROWS - 1) * (2 * NA_WIN_COLS - 1)
    return pl.pallas_call(
        _na_bias_kernel,
        grid=(NA_HEADS,),
        in_specs=[pl.BlockSpec(memory_space=pltpu.SMEM)],
        out_specs=pl.BlockSpec((None, 3, NA_TQ, NA_TK), lambda h: (h, 0, 0, 0)),
        out_shape=jax.ShapeDtypeStruct((NA_HEADS, 3, NA_TQ, NA_TK), _F32),
        compiler_params=pltpu.CompilerParams(dimension_semantics=("parallel",)),
        name="na_bias",
    )(rpb.reshape(NA_HEADS, n_tab))


def _na_attn_kernel(q_ref, k_ref, v_ref, bias_ref, o_ref, *, rows):
    last = pl.num_programs(2) * NA_BLOCKS_PER_STEP - 1
    for u in range(NA_BLOCKS_PER_STEP):
        r = pl.program_id(2) * NA_BLOCKS_PER_STEP + u
        kr0 = jnp.clip(r * NA_Q_ROWS - NA_WIN_ROWS // 2, 0, rows - NA_K_ROWS)
        start = pl.multiple_of(kr0 * GRID_W, GRID_W)
        kb = k_ref[pl.ds(start, NA_TK), :]
        vb = v_ref[pl.ds(start, NA_TK), :]
        variant = jnp.where(r == 0, 0, jnp.where(r == last, 2, 1))
        q = q_ref[u * NA_TQ:(u + 1) * NA_TQ, :]
        s = lax.dot_general(q, kb, _NT, preferred_element_type=_F32)
        s = s * (NA_HD ** -0.5) + bias_ref[variant]
        p = jnp.exp(s - jnp.max(s, axis=-1, keepdims=True))
        l = jnp.sum(p, axis=-1, keepdims=True)
        o = jnp.dot(p.astype(_BF16), vb, preferred_element_type=_F32) / l
        o_ref[u * NA_TQ:(u + 1) * NA_TQ, :] = o.astype(o_ref.dtype)


def _na_attn(proj, bias, *, col_q, col_k, col_v):
    b, t, _ = proj.shape
    rows = t // GRID_W
    tq = NA_BLOCKS_PER_STEP * NA_TQ
    assert t % tq == 0 and rows >= NA_K_ROWS + NA_Q_ROWS
    limit = _vmem_limit(2 * _nbytes((t, NA_HD), _BF16), _nbytes((3, NA_TQ, NA_TK), _F32),
                        2 * _nbytes((tq, NA_HD), _BF16),
                        2 * NA_BLOCKS_PER_STEP * _nbytes((NA_TQ, NA_TK), _F32))
    return pl.pallas_call(
        functools.partial(_na_attn_kernel, rows=rows),
        grid=(b, NA_HEADS, t // tq),
        in_specs=[
            pl.BlockSpec((None, tq, NA_HD), lambda bi, h, r: (bi, r, col_q + h)),
            pl.BlockSpec((None, t, NA_HD), lambda bi, h, r: (bi, 0, col_k + h)),
            pl.BlockSpec((None, t, NA_HD), lambda bi, h, r: (bi, 0, col_v + h)),
            pl.BlockSpec((None, 3, NA_TQ, NA_TK), lambda bi, h, r: (h, 0, 0, 0)),
        ],
        out_specs=pl.BlockSpec((None, tq, NA_HD), lambda bi, h, r: (bi, r, h)),
        out_shape=jax.ShapeDtypeStruct((b, t, NA_HEADS * NA_HD), _BF16),
        compiler_params=pltpu.CompilerParams(
            dimension_semantics=("parallel", "parallel", "arbitrary"), vmem_limit_bytes=limit),
        name="na_attn",
    )(proj, proj, proj, bias)


def _diff_attn_kernel(q_ref, k_ref, v_ref, lq1_ref, lk1_ref, lq2_ref, lk2_ref, g_ref, o_ref,
                      vt_ref, acc1_ref, acc2_ref, *, tk, chunks_per_trip):
    n_chunks = vt_ref.shape[0]

    @pl.when(pl.program_id(2) == 0)
    def _transpose_values():
        row = lax.broadcasted_iota(jnp.int32, (BF16_SUBLANES, tk), 0)
        ones_tile = jnp.where(row == 0, 1.0, 0.0).astype(vt_ref.dtype)

        def body(c, carry):
            off = pl.multiple_of(c * tk, tk)
            vt_ref[c, :DIFF_DV, :] = v_ref[pl.ds(off, tk), :].astype(_F32).T.astype(vt_ref.dtype)
            vt_ref[c, DIFF_DV:, :] = ones_tile
            return carry
        lax.fori_loop(0, n_chunks, body, 0)

    q = q_ref[...]
    tq = q.shape[0]
    lane = lax.broadcasted_iota(jnp.int32, q.shape, 1)
    zero = jnp.zeros_like(q)
    qa = jnp.where(lane < DIFF_DK, q, zero)
    qb = jnp.where(lane >= DIFF_DK, q, zero)
    acc_refs = (acc1_ref, acc2_ref)

    def chunk_scores(c):
        off = pl.multiple_of(c * tk, tk)
        kc = k_ref[pl.ds(off, tk), :]
        return [lax.dot_general(kc, qh, _NT, preferred_element_type=_F32) for qh in (qa, qb)]

    def weighted_values(c, s, m):
        return jnp.dot(vt_ref[c], jnp.exp2(s - m).astype(_BF16), preferred_element_type=_F32)

    def first_chunk():
        m = []
        for s, acc_ref in zip(chunk_scores(0), acc_refs):
            m.append(jnp.max(s, axis=0, keepdims=True))
            acc_ref[...] = weighted_values(0, s, m[-1])
        return tuple(m)

    def lazy_chunk(c, state):
        new_state = []
        for s, acc_ref, (m, lag) in zip(chunk_scores(c), acc_refs, state):
            pv = weighted_values(c, s, m)
            chunk_max = jnp.max(s, axis=0, keepdims=True)
            m_new = jnp.maximum(m, chunk_max)
            acc_ref[...] = (acc_ref[...] + pv) * jnp.exp2(m - m_new)
            new_state.append((m_new, jnp.maximum(lag, chunk_max - m)))
        return tuple(new_state)

    def max_first_chunk(c, m_old):
        m_out = []
        for s, acc_ref, m in zip(chunk_scores(c), acc_refs, m_old):
            m_new = jnp.maximum(m, jnp.max(s, axis=0, keepdims=True))
            acc_ref[...] = jnp.exp2(m - m_new) * acc_ref[...] + weighted_values(c, s, m_new)
            m_out.append(m_new)
        return tuple(m_out)

    n_trips, n_tail = divmod(n_chunks - 1, chunks_per_trip)

    def fast_body(j, state):
        for u in range(chunks_per_trip):
            state = lazy_chunk(1 + j * chunks_per_trip + u, state)
        return state

    no_lag = jnp.zeros((1, tq), _F32)
    state = lax.fori_loop(0, n_trips, fast_body, tuple((m, no_lag) for m in first_chunk()))
    for c in range(n_chunks - n_tail, n_chunks):
        state = lazy_chunk(c, state)
    worst_lag = jnp.max(jnp.maximum(state[0][1], state[1][1]))

    @pl.when(jnp.logical_not(worst_lag <= DIFF_MAX_LAG))
    def _slow_path():
        lax.fori_loop(1, n_chunks, max_first_chunk, first_chunk())

    lam = (jnp.exp(jnp.sum(lq1_ref[...] * lk1_ref[...], axis=-1, keepdims=True))
           - jnp.exp(jnp.sum(lq2_ref[...] * lk2_ref[...], axis=-1, keepdims=True)) + LAM_INIT)
    o1 = acc1_ref[:DIFF_DV, :] / acc1_ref[DIFF_DV:DIFF_DV + 1, :]
    o2 = acc2_ref[:DIFF_DV, :] / acc2_ref[DIFF_DV:DIFF_DV + 1, :]
    o = o1 - lam * o2
    inv = lax.rsqrt(jnp.mean(o * o, axis=0, keepdims=True) + SUBLN_EPS)
    o = ((o * inv) * g_ref[...]) * (1.0 - LAM_INIT)
    o_ref[...] = o.T.astype(o_ref.dtype)


def _diff_attn(proj, lam_q1, lam_k1, lam_q2, lam_k2, g_subln, *, col_q, col_k, col_v, tq, tk):
    b, t, _ = proj.shape
    assert t % tq == 0 and t % tk == 0 and t // tk >= 2
    lam_spec = pl.BlockSpec((1, DIFF_DK), lambda bi, h, i: (0, 0))
    limit = _vmem_limit(2 * _nbytes((t, LANES), _BF16), 2 * _nbytes((tq, LANES), _BF16),
                        _nbytes((t, LANES), _BF16) // 2, _nbytes((DIFF_DV, tq), _F32),
                        2 * _nbytes((tk, tq), _F32))
    return pl.pallas_call(
        functools.partial(_diff_attn_kernel, tk=tk, chunks_per_trip=DIFF_CHUNKS_PER_TRIP),
        grid=(b, DIFF_HEADS, t // tq),
        in_specs=[
            pl.BlockSpec((None, tq, LANES), lambda bi, h, i: (bi, i, col_q + h)),
            pl.BlockSpec((None, t, LANES), lambda bi, h, i: (bi, 0, col_k + h)),
            pl.BlockSpec((None, t, LANES), lambda bi, h, i: (bi, 0, col_v + h)),
            lam_spec, lam_spec, lam_spec, lam_spec,
            pl.BlockSpec((DIFF_DV, 1), lambda bi, h, i: (0, 0)),
        ],
        out_specs=pl.BlockSpec((None, tq, DIFF_DV), lambda bi, h, i: (bi, i, h)),
        out_shape=jax.ShapeDtypeStruct((b, t, DIFF_HEADS * DIFF_DV), _BF16),
        scratch_shapes=[
            pltpu.VMEM((t // tk, DIFF_ACC_ROWS, tk), _BF16),
            pltpu.VMEM((DIFF_ACC_ROWS, tq), _F32),
            pltpu.VMEM((DIFF_ACC_ROWS, tq), _F32),
        ],
        compiler_params=pltpu.CompilerParams(
            dimension_semantics=("parallel", "parallel", "arbitrary"), vmem_limit_bytes=limit),
        name="diff_attn",
    )(proj, proj, proj, lam_q1.reshape(1, DIFF_DK), lam_k1.reshape(1, DIFF_DK),
      lam_q2.reshape(1, DIFF_DK), lam_k2.reshape(1, DIFF_DK), g_subln.reshape(DIFF_DV, 1))


def _out_proj_kernel(x_ref, na_ref, df_ref, w_ref, o_ref):
    n_na = na_ref.shape[1]
    acc = jnp.dot(na_ref[...], w_ref[:n_na, :], preferred_element_type=_F32)
    acc += jnp.dot(df_ref[...], w_ref[n_na:, :], preferred_element_type=_F32)
    o_ref[...] = x_ref[...] + acc


def _out_proj(x, na_o, df_o, w_out, *, tm, tn):
    m, d = x.shape
    k_na, k_df = na_o.shape[1], df_o.shape[1]
    assert m % tm == 0 and d % tn == 0 and w_out.shape == (k_na + k_df, d)
    limit = _vmem_limit(2 * _nbytes((tm, tn), _F32), _nbytes((tm, k_na + k_df), _BF16),
                        _nbytes((k_na + k_df, tn), _BF16))
    return pl.pallas_call(
        _out_proj_kernel,
        grid=(m // tm, d // tn),
        in_specs=[
            pl.BlockSpec((tm, tn), lambda i, j: (i, j)),
            pl.BlockSpec((tm, k_na), lambda i, j: (i, 0)),
            pl.BlockSpec((tm, k_df), lambda i, j: (i, 0)),
            pl.BlockSpec((k_na + k_df, tn), lambda i, j: (0, j)),
        ],
        out_specs=pl.BlockSpec((tm, tn), lambda i, j: (i, j)),
        out_shape=jax.ShapeDtypeStruct((m, d), _F32),
        compiler_params=pltpu.CompilerParams(
            dimension_semantics=("parallel", "parallel"), vmem_limit_bytes=limit),
        name="out_proj",
    )(x, na_o, df_o, w_out)


def _mem_attn_kernel(q_ref, kv_ref, h_ref, w_ref, o_ref, ctx_ref):
    d = q_ref.shape[1]
    hd = d // MEM_HEADS
    for h in range(MEM_HEADS):
        qh = q_ref[:, h * hd:(h + 1) * hd]
        kh = kv_ref[:, h * hd:(h + 1) * hd]
        vh = kv_ref[:, d + h * hd:d + (h + 1) * hd]
        s = lax.dot_general(qh, kh, _NT, preferred_element_type=_F32) * (hd ** -0.5)
        p = jnp.exp(s - jnp.max(s, axis=-1, keepdims=True))
        l = jnp.sum(p, axis=-1, keepdims=True)
        ctx = jnp.dot(p.astype(_BF16), vh, preferred_element_type=_F32) / l
        ctx_ref[:, h * hd:(h + 1) * hd] = ctx.astype(ctx_ref.dtype)
    o_ref[...] = h_ref[...] + jnp.dot(ctx_ref[...], w_ref[...], preferred_element_type=_F32)


def _mem_attn(q, kv, h1, w_mo, *, seq_len, mem_tokens, tm):
    m, d = h1.shape
    assert m % tm == 0 and seq_len % tm == 0
    tiles_per_seq = seq_len // tm
    limit = _vmem_limit(_nbytes((tm, d), _BF16), _nbytes((mem_tokens, 2 * d), _BF16),
                        2 * _nbytes((tm, d), _F32), _nbytes((d, d), _BF16),
                        _nbytes((tm, d), _BF16) // 2)
    return pl.pallas_call(
        _mem_attn_kernel,
        grid=(m // tm,),
        in_specs=[
            pl.BlockSpec((tm, d), lambda i: (i, 0)),
            pl.BlockSpec((mem_tokens, 2 * d), lambda i: (i // tiles_per_seq, 0)),
            pl.BlockSpec((tm, d), lambda i: (i, 0)),
            pl.BlockSpec((d, d), lambda i: (0, 0)),
        ],
        out_specs=pl.BlockSpec((tm, d), lambda i: (i, 0)),
        out_shape=jax.ShapeDtypeStruct((m, d), _F32),
        scratch_shapes=[pltpu.VMEM((tm, d), _BF16)],
        compiler_params=pltpu.CompilerParams(
            dimension_semantics=("parallel",), vmem_limit_bytes=limit),
        name="mem_attn",
    )(q, kv, h1, w_mo)


def _ffn_kernel(h_ref, g_ref, wg_ref, wu_ref, wd_ref, gf_ref, o_ref, hn_ref, acc_ref):
    f = pl.program_id(1)

    @pl.when(f == 0)
    def _normalize():
        x = h_ref[...]
        inv = lax.rsqrt(jnp.mean(x * x, axis=-1, keepdims=True) + RMS_EPS)
        hn_ref[...] = ((x * inv) * g_ref[...]).astype(hn_ref.dtype)
        acc_ref[...] = jnp.zeros_like(acc_ref)

    hn = hn_ref[...]
    gate = jnp.dot(hn, wg_ref[...], preferred_element_type=_F32)
    up = jnp.dot(hn, wu_ref[...], preferred_element_type=_F32)
    act = (gate * jax.nn.sigmoid(gate)) * up
    acc_ref[...] += jnp.dot(act.astype(_BF16), wd_ref[...], preferred_element_type=_F32)

    @pl.when(f == pl.num_programs(1) - 1)
    def _finish():
        y = h_ref[...] + acc_ref[...]
        inv = lax.rsqrt(jnp.mean(y * y, axis=-1, keepdims=True) + RMS_EPS)
        o_ref[...] = (y * inv) * gf_ref[...]


def _ffn(h2, g_ffn, w_gate_up, w_down, g_final, *, tm, tf):
    m, d = h2.shape
    d_ff = w_down.shape[0]
    assert m % tm == 0 and d_ff % tf == 0 and w_gate_up.shape == (d, 2 * d_ff)
    n_f = d_ff // tf
    limit = _vmem_limit(2 * _nbytes((tm, d), _F32), 3 * _nbytes((d, tf), _BF16),
                        _nbytes((tm, d), _BF16) // 2, _nbytes((tm, d), _F32) // 2,
                        2 * _nbytes((tm, tf), _F32))
    return pl.pallas_call(
        _ffn_kernel,
        grid=(m // tm, n_f),
        in_specs=[
            pl.BlockSpec((tm, d), lambda i, f: (i, 0)),
            pl.BlockSpec((1, d), lambda i, f: (0, 0)),
            pl.BlockSpec((d, tf), lambda i, f: (0, f)),
            pl.BlockSpec((d, tf), lambda i, f: (0, n_f + f)),
            pl.BlockSpec((tf, d), lambda i, f: (f, 0)),
            pl.BlockSpec((1, d), lambda i, f: (0, 0)),
        ],
        out_specs=pl.BlockSpec((tm, d), lambda i, f: (i, 0)),
        out_shape=jax.ShapeDtypeStruct((m, d), _F32),
        scratch_shapes=[pltpu.VMEM((tm, d), _BF16), pltpu.VMEM((tm, d), _F32)],
        compiler_params=pltpu.CompilerParams(
            dimension_semantics=("parallel", "arbitrary"), vmem_limit_bytes=limit),
        name="ffn",
    )(h2, g_ffn.reshape(1, d), w_gate_up, w_gate_up, w_down, g_final.reshape(1, d))


def _trunk(x, mem, p, na_bias):
    b, t, d = x.shape
    mem_tokens = mem.shape[1]
    m = b * t
    x2 = x.reshape(m, d)
    na_w = NA_HEADS * NA_HD
    qk_w = DIFF_HEADS * 2 * DIFF_DK
    tn_in = 512
    rope_lo = 3 * na_w // tn_in
    rope = (_rope_tables(t), (rope_lo, rope_lo + qk_w // tn_in, rope_lo + 2 * qk_w // tn_in))
    proj = _norm_matmul(x2, p["g_mix"], p["w_in"], tm=1024, tn=tn_in, seq_len=t, rope=rope,
                        name="in_proj")
    proj = proj.reshape(b, t, -1)
    blk = lambda col: col // LANES
    na_o = _na_attn(proj, na_bias, col_q=0, col_k=blk(na_w), col_v=blk(2 * na_w))
    df_o = _diff_attn(proj, p["lam_q1"], p["lam_k1"], p["lam_q2"], p["lam_k2"], p["g_subln"],
                      col_q=blk(3 * na_w), col_k=blk(3 * na_w + qk_w),
                      col_v=blk(3 * na_w + 2 * qk_w), tq=512, tk=512)
    h1 = _out_proj(x2, na_o.reshape(m, -1), df_o.reshape(m, -1), p["w_out"], tm=1024, tn=512)
    q_mem = _norm_matmul(h1, p["g_xattn"], p["w_mq"], tm=1024, tn=512, name="mem_q")
    kv_mem = _norm_matmul(mem.reshape(b * mem_tokens, d), p["g_mem"], p["w_mkv"],
                          tm=min(512, b * mem_tokens), tn=512, name="mem_kv")
    h2 = _mem_attn(q_mem, kv_mem, h1, p["w_mo"], seq_len=t, mem_tokens=mem_tokens, tm=512)
    y = _ffn(h2, p["g_ffn"], p["w_gate_up"], p["w_down"], p["g_final"], tm=512, tf=512)
    return y.reshape(b, t, d)


def kernel(x_prompt, x_sample, mem_prompt, mem_sample, g_mix, w_in, rpb, lam_q1, lam_k1,
           lam_q2, lam_k2, g_subln, w_out, g_xattn, g_mem, w_mq, w_mkv, w_mo, g_ffn,
           w_gate_up, w_down, g_final):
    assert w_in.shape[0] == 1, "single-layer trunk"
    p = dict(
        g_mix=g_mix[0], w_in=w_in[0].astype(_BF16), lam_q1=lam_q1[0], lam_k1=lam_k1[0],
        lam_q2=lam_q2[0], lam_k2=lam_k2[0], g_subln=g_subln[0], w_out=w_out[0].astype(_BF16),
        g_xattn=g_xattn[0], g_mem=g_mem[0], w_mq=w_mq[0].astype(_BF16),
        w_mkv=w_mkv[0].astype(_BF16), w_mo=w_mo[0].astype(_BF16), g_ffn=g_ffn[0],
        w_gate_up=w_gate_up[0].astype(_BF16), w_down=w_down[0].astype(_BF16), g_final=g_final,
    )
    na_bias = _na_bias(rpb[0])
    return (_trunk(x_prompt, mem_prompt, p, na_bias), _trunk(x_sample, mem_sample, p, na_bias))
```

```python
import functools
import math

import jax
import jax.numpy as jnp
import numpy as np
from jax import lax
from jax.experimental import pallas as pl
from jax.experimental.pallas import tpu as pltpu

GRID_W = 64
NA_HEADS = 8
NA_HD = 128
NA_WIN_ROWS = 8
NA_WIN_COLS = 16
DIFF_HEADS = 8
DIFF_DK = 64
DIFF_DV = 2 * DIFF_DK
ROPE_THETA = 10000.0
MEM_HEADS = 4
RMS_EPS = 1e-6
SUBLN_EPS = 1e-5
LAM_INIT = 0.8 - 0.6 * math.exp(-0.3 * 0)
DIFF_Q_SCALE = DIFF_DK ** -0.5 * math.log2(math.e)
BF16_SUBLANES = 16
DIFF_ACC_ROWS = DIFF_DV + BF16_SUBLANES
DIFF_CHUNKS_PER_TRIP = 8
DIFF_MAX_LAG = 64.0

LANES = 128
MXU_COLS = 256
V7X_VMEM_BYTES = 64 * 1024 * 1024
VMEM_LIMIT_CAP = V7X_VMEM_BYTES - 6 * 1024 * 1024
INTERNAL_SCRATCH_ALLOWANCE = 12 * 1024 * 1024

NA_Q_ROWS = 4
NA_K_ROWS = NA_Q_ROWS + NA_WIN_ROWS
NA_TQ = NA_Q_ROWS * GRID_W
NA_TK = NA_K_ROWS * GRID_W
NA_BLOCKS_PER_STEP = 8
ROPE_ROW_GROUP = 256
MASK_VALUE = -1e30

_NT = (((1,), (1,)), ((), ()))

_BF16 = jnp.bfloat16
_F32 = jnp.float32


def _vmem_limit(*block_bytes):
    return int(min(VMEM_LIMIT_CAP, 2 * sum(block_bytes) + INTERNAL_SCRATCH_ALLOWANCE))


def _nbytes(shape, dtype):
    return int(np.prod(shape)) * jnp.dtype(dtype).itemsize


def _norm_matmul_kernel(x_ref, g_ref, w_ref, *rest, rope_blocks, q_scale):
    def normalized():
        x = x_ref[...]
        inv = lax.rsqrt(jnp.mean(x * x, axis=-1, keepdims=True) + RMS_EPS)
        return ((x * inv) * g_ref[...]).astype(_BF16)

    if len(rest) == 1:
        (o_ref,) = rest
        acc = jnp.dot(normalized(), w_ref[...], preferred_element_type=_F32)
        o_ref[...] = acc.astype(o_ref.dtype)
        return

    if rope_blocks is None:
        o_ref, xn_ref = rest
    else:
        cos_ref, sin_up_ref, sin_dn_ref, o_ref, xn_ref = rest
    j = pl.program_id(1)

    @pl.when(j == 0)
    def _first():
        xn = normalized()
        xn_ref[...] = xn
        o_ref[...] = jnp.dot(xn, w_ref[...], preferred_element_type=_F32).astype(o_ref.dtype)

    def _plain():
        acc = jnp.dot(xn_ref[...], w_ref[...], preferred_element_type=_F32)
        o_ref[...] = acc.astype(o_ref.dtype)

    if rope_blocks is None:
        pl.when(j > 0)(_plain)
        return

    lo, q_hi, hi = rope_blocks
    assert lo > 0
    is_rope = jnp.logical_and(j >= lo, j < hi)
    pl.when(jnp.logical_and(j > 0, jnp.logical_not(is_rope)))(_plain)

    @pl.when(is_rope)
    def _rotary():
        scale = jnp.where(j < q_hi, q_scale, 1.0).astype(_F32)
        tm, tn = o_ref.shape
        rows = min(tm, ROPE_ROW_GROUP)
        for r0 in range(0, tm, rows):
            rs = slice(r0, r0 + rows)
            cos, sin_up, sin_dn = cos_ref[rs, :], sin_up_ref[rs, :], sin_dn_ref[rs, :]
            xn = xn_ref[rs, :]
            for n0 in range(0, tn, MXU_COLS):
                acc = jnp.dot(xn, w_ref[:, n0:n0 + MXU_COLS], preferred_element_type=_F32)
                for c0 in range(0, MXU_COLS, LANES):
                    a = acc[:, c0:c0 + LANES]
                    r = (a * cos + pltpu.roll(a, LANES - DIFF_DK // 2, 1) * sin_up
                         + pltpu.roll(a, DIFF_DK // 2, 1) * sin_dn)
                    o_ref[rs, n0 + c0:n0 + c0 + LANES] = (r * scale).astype(o_ref.dtype)


def _norm_matmul(x, g, w, *, tm, tn, seq_len=None, rope=None, name):
    m, d = x.shape
    n = w.shape[1]
    assert m % tm == 0 and n % tn == 0
    in_specs = [
        pl.BlockSpec((tm, d), lambda i, j: (i, 0)),
        pl.BlockSpec((1, d), lambda i, j: (0, 0)),
        pl.BlockSpec((d, tn), lambda i, j: (0, j)),
    ]
    args = [x, g.reshape(1, d), w]
    rope_blocks = None
    if rope is not None:
        tables, rope_blocks = rope
        assert seq_len % tm == 0
        tiles_per_seq = seq_len // tm
        for t in tables:
            in_specs.append(pl.BlockSpec((tm, LANES), lambda i, j: (i % tiles_per_seq, 0)))
            args.append(t)
    kernel = functools.partial(_norm_matmul_kernel, rope_blocks=rope_blocks,
                               q_scale=DIFF_Q_SCALE)
    limit = _vmem_limit(_nbytes((tm, d), _F32), _nbytes((d, tn), _BF16), _nbytes((tm, tn), _BF16),
                        3 * _nbytes((tm, LANES), _F32), _nbytes((tm, d), _BF16) // 2,
                        _nbytes((tm, tn), _F32))
    return pl.pallas_call(
        kernel,
        grid=(m // tm, n // tn),
        in_specs=in_specs,
        out_specs=pl.BlockSpec((tm, tn), lambda i, j: (i, j)),
        out_shape=jax.ShapeDtypeStruct((m, n), _BF16),
        scratch_shapes=[] if n == tn and rope is None else [pltpu.VMEM((tm, d), _BF16)],
        compiler_params=pltpu.CompilerParams(
            dimension_semantics=("parallel", "arbitrary"), vmem_limit_bytes=limit),
        name=name,
    )(*args)


def _rope_tables(seq_len):
    inv = 1.0 / (ROPE_THETA ** (jnp.arange(0, DIFF_DK, 2, dtype=_F32) / DIFF_DK))
    ang = jnp.arange(seq_len, dtype=_F32)[:, None] * inv[None, :]
    ang = jnp.tile(ang, (1, LANES // (DIFF_DK // 2)))
    cos, sin = jnp.cos(ang), jnp.sin(ang)
    first_half = jnp.asarray((np.arange(LANES) % DIFF_DK) < DIFF_DK // 2)[None, :]
    sin_up = jnp.where(first_half, -sin, 0.0)
    sin_dn = jnp.where(first_half, 0.0, sin)
    return cos, sin_up, sin_dn


def _na_window(variant, i):
    if variant == 0:
        return 0, NA_WIN_ROWS - 1 - i
    if variant == 1:
        return i, NA_WIN_ROWS // 2 - 1 - i
    return NA_K_ROWS - NA_WIN_ROWS, -1 - i


def _na_bias_kernel(rpb_ref, o_ref):
    h = pl.program_id(0)
    n_dc = 2 * NA_WIN_COLS - 1
    c = lax.broadcasted_iota(jnp.int32, (GRID_W, LANES), 0)
    lane = lax.broadcasted_iota(jnp.int32, (GRID_W, LANES), 1)
    kc = lane & (GRID_W - 1)
    cs = jnp.clip(c - NA_WIN_COLS // 2, 0, GRID_W - NA_WIN_COLS)
    col_ok = jnp.logical_and(kc >= cs, kc < cs + NA_WIN_COLS)
    dc = kc - c + (NA_WIN_COLS - 1)
    neg = jnp.full((GRID_W, LANES), MASK_VALUE, _F32)
    by_dr = []
    for d in range(2 * NA_WIN_ROWS - 1):
        u = neg
        for n in range(n_dc):
            u = jnp.where(dc == n, rpb_ref[h, d * n_dc + n], u)
        by_dr.append(jnp.where(col_ok, u, neg))
    low_half = lane < GRID_W
    for variant in range(3):
        for i in range(NA_Q_ROWS):
            first, dr0 = _na_window(variant, i)
            for t in range(NA_K_ROWS // 2):
                halves = []
                for jj in (2 * t, 2 * t + 1):
                    valid = first <= jj < first + NA_WIN_ROWS
                    halves.append(by_dr[dr0 + jj] if valid else neg)
                tile = (halves[0] if halves[0] is halves[1]
                        else jnp.where(low_half, halves[0], halves[1]))
                o_ref[variant, i * GRID_W:(i + 1) * GRID_W, t * LANES:(t + 1) * LANES] = tile


def _na_bias(rpb):
    n_tab = (2 * NA_WIN_ROWS - 1) * (2 * NA_WIN_COLS - 1)
    return pl.pallas_call(
        _na_bias_kernel,
        grid=(NA_HEADS,),
        in_specs=[pl.BlockSpec(memory_space=pltpu.SMEM)],
        out_specs=pl.BlockSpec((None, 3, NA_TQ, NA_TK), lambda h: (h, 0, 0, 0)),
        out_shape=jax.ShapeDtypeStruct((NA_HEADS, 3, NA_TQ, NA_TK), _F32),
        compiler_params=pltpu.CompilerParams(dimension_semantics=("parallel",)),
        name="na_bias",
    )(rpb.reshape(NA_HEADS, n_tab))


def _na_attn_kernel(q_ref, k_ref, v_ref, bias_ref, o_ref, *, rows):
    last = pl.num_programs(2) * NA_BLOCKS_PER_STEP - 1
    for u in range(NA_BLOCKS_PER_STEP):
        r = pl.program_id(2) * NA_BLOCKS_PER_STEP + u
        kr0 = jnp.clip(r * NA_Q_ROWS - NA_WIN_ROWS // 2, 0, rows - NA_K_ROWS)
        start = pl.multiple_of(kr0 * GRID_W, GRID_W)
        kb = k_ref[pl.ds(start, NA_TK), :]
        vb = v_ref[pl.ds(start, NA_TK), :]
        variant = jnp.where(r == 0, 0, jnp.where(r == last, 2, 1))
        q = q_ref[u * NA_TQ:(u + 1) * NA_TQ, :]
        s = lax.dot_general(q, kb, _NT, preferred_element_type=_F32)
        s = s * (NA_HD ** -0.5) + bias_ref[variant]
        p = jnp.exp(s - jnp.max(s, axis=-1, keepdims=True))
        l = jnp.sum(p, axis=-1, keepdims=True)
        o = jnp.dot(p.astype(_BF16), vb, preferred_element_type=_F32) / l
        o_ref[u * NA_TQ:(u + 1) * NA_TQ, :] = o.astype(o_ref.dtype)


def _na_attn(proj, bias, *, col_q, col_k, col_v):
    b, t, _ = proj.shape
    rows = t // GRID_W
    tq = NA_BLOCKS_PER_STEP * NA_TQ
    assert t % tq == 0 and rows >= NA_K_ROWS + NA_Q_ROWS
    limit = _vmem_limit(2 * _nbytes((t, NA_HD), _BF16), _nbytes((3, NA_TQ, NA_TK), _F32),
                        2 * _nbytes((tq, NA_HD), _BF16),
                        2 * NA_BLOCKS_PER_STEP * _nbytes((NA_TQ, NA_TK), _F32))
    return pl.pallas_call(
        functools.partial(_na_attn_kernel, rows=rows),
        grid=(b, NA_HEADS, t // tq),
        in_specs=[
            pl.BlockSpec((None, tq, NA_HD), lambda bi, h, r: (bi, r, col_q + h)),
            pl.BlockSpec((None, t, NA_HD), lambda bi, h, r: (bi, 0, col_k + h)),
            pl.BlockSpec((None, t, NA_HD), lambda bi, h, r: (bi, 0, col_v + h)),
            pl.BlockSpec((None, 3, NA_TQ, NA_TK), lambda bi, h, r: (h, 0, 0, 0)),
        ],
        out_specs=pl.BlockSpec((None, tq, NA_HD), lambda bi, h, r: (bi, r, h)),
        out_shape=jax.ShapeDtypeStruct((b, t, NA_HEADS * NA_HD), _BF16),
        compiler_params=pltpu.CompilerParams(
            dimension_semantics=("parallel", "parallel", "arbitrary"), vmem_limit_bytes=limit),
        name="na_attn",
    )(proj, proj, proj, bias)


def _diff_attn_kernel(q_ref, k_ref, v_ref, lq1_ref, lk1_ref, lq2_ref, lk2_ref, g_ref, o_ref,
                      vt_ref, acc1_ref, acc2_ref, *, tk, chunks_per_trip):
    n_chunks = vt_ref.shape[0]

    @pl.when(pl.program_id(2) == 0)
    def _transpose_values():
        row = lax.broadcasted_iota(jnp.int32, (BF16_SUBLANES, tk), 0)
        ones_tile = jnp.where(row == 0, 1.0, 0.0).astype(vt_ref.dtype)

        def body(c, carry):
            off = pl.multiple_of(c * tk, tk)
            vt_ref[c, :DIFF_DV, :] = v_ref[pl.ds(off, tk), :].astype(_F32).T.astype(vt_ref.dtype)
            vt_ref[c, DIFF_DV:, :] = ones_tile
            return carry
        lax.fori_loop(0, n_chunks, body, 0)

    q = q_ref[...]
    tq = q.shape[0]
    lane = lax.broadcasted_iota(jnp.int32, q.shape, 1)
    zero = jnp.zeros_like(q)
    qa = jnp.where(lane < DIFF_DK, q, zero)
    qb = jnp.where(lane >= DIFF_DK, q, zero)
    acc_refs = (acc1_ref, acc2_ref)

    def chunk_scores(c):
        off = pl.multiple_of(c * tk, tk)
        kc = k_ref[pl.ds(off, tk), :]
        return [lax.dot_general(kc, qh, _NT, preferred_element_type=_F32) for qh in (qa, qb)]

    def weighted_values(c, s, m):
        return jnp.dot(vt_ref[c], jnp.exp2(s - m).astype(_BF16), preferred_element_type=_F32)

    def first_chunk():
        m = []
        for s, acc_ref in zip(chunk_scores(0), acc_refs):
            m.append(jnp.max(s, axis=0, keepdims=True))
            acc_ref[...] = weighted_values(0, s, m[-1])
        return tuple(m)

    def lazy_chunk(c, state):
        new_state = []
        for s, acc_ref, (top, lag) in zip(chunk_scores(c), acc_refs, state):
            m = jnp.maximum(top, 0.0)
            pv = weighted_values(c, s, m)
            chunk_max = jnp.max(s, axis=0, keepdims=True)
            acc_ref[...] = (acc_ref[...] + pv) * jnp.exp2(m - jnp.maximum(m, chunk_max))
            new_state.append((jnp.maximum(top, chunk_max), jnp.maximum(lag, chunk_max - m)))
        return tuple(new_state)

    def max_first_chunk(c, m_old):
        m_out = []
        for s, acc_ref, m in zip(chunk_scores(c), acc_refs, m_old):
            m_new = jnp.maximum(m, jnp.max(s, axis=0, keepdims=True))
            acc_ref[...] = jnp.exp2(m - m_new) * acc_ref[...] + weighted_values(c, s, m_new)
            m_out.append(m_new)
        return tuple(m_out)

    n_trips, n_tail = divmod(n_chunks, chunks_per_trip)

    def fast_body(j, state):
        for u in range(chunks_per_trip):
            state = lazy_chunk(j * chunks_per_trip + u, state)
        return state

    for acc_ref in acc_refs:
        acc_ref[...] = jnp.zeros_like(acc_ref)
    start = (jnp.full((1, tq), MASK_VALUE, _F32), jnp.zeros((1, tq), _F32))
    state = lax.fori_loop(0, n_trips, fast_body, (start, start))
    for c in range(n_chunks - n_tail, n_chunks):
        state = lazy_chunk(c, state)
    worst_lag = jnp.max(jnp.maximum(jnp.maximum(state[0][1], -state[0][0]),
                                    jnp.maximum(state[1][1], -state[1][0])))

    @pl.when(jnp.logical_not(worst_lag <= DIFF_MAX_LAG))
    def _slow_path():
        lax.fori_loop(1, n_chunks, max_first_chunk, first_chunk())

    lam = (jnp.exp(jnp.sum(lq1_ref[...] * lk1_ref[...], axis=-1, keepdims=True))
           - jnp.exp(jnp.sum(lq2_ref[...] * lk2_ref[...], axis=-1, keepdims=True)) + LAM_INIT)
    o1 = acc1_ref[:DIFF_DV, :] / acc1_ref[DIFF_DV:DIFF_DV + 1, :]
    o2 = acc2_ref[:DIFF_DV, :] / acc2_ref[DIFF_DV:DIFF_DV + 1, :]
    o = o1 - lam * o2
    inv = lax.rsqrt(jnp.mean(o * o, axis=0, keepdims=True) + SUBLN_EPS)
    o = ((o * inv) * g_ref[...]) * (1.0 - LAM_INIT)
    o_ref[...] = o.T.astype(o_ref.dtype)


def _diff_attn(proj, lam_q1, lam_k1, lam_q2, lam_k2, g_subln, *, col_q, col_k, col_v, tq, tk):
    b, t, _ = proj.shape
    assert t % tq == 0 and t % tk == 0 and t // tk >= 2
    lam_spec = pl.BlockSpec((1, DIFF_DK), lambda bi, h, i: (0, 0))
    limit = _vmem_limit(2 * _nbytes((t, LANES), _BF16), 2 * _nbytes((tq, LANES), _BF16),
                        _nbytes((t, LANES), _BF16) // 2, _nbytes((DIFF_DV, tq), _F32),
                        2 * _nbytes((tk, tq), _F32))
    return pl.pallas_call(
        functools.partial(_diff_attn_kernel, tk=tk, chunks_per_trip=DIFF_CHUNKS_PER_TRIP),
        grid=(b, DIFF_HEADS, t // tq),
        in_specs=[
            pl.BlockSpec((None, tq, LANES), lambda bi, h, i: (bi, i, col_q + h)),
            pl.BlockSpec((None, t, LANES), lambda bi, h, i: (bi, 0, col_k + h)),
            pl.BlockSpec((None, t, LANES), lambda bi, h, i: (bi, 0, col_v + h)),
            lam_spec, lam_spec, lam_spec, lam_spec,
            pl.BlockSpec((DIFF_DV, 1), lambda bi, h, i: (0, 0)),
        ],
        out_specs=pl.BlockSpec((None, tq, DIFF_DV), lambda bi, h, i: (bi, i, h)),
        out_shape=jax.ShapeDtypeStruct((b, t, DIFF_HEADS * DIFF_DV), _BF16),
        scratch_shapes=[
            pltpu.VMEM((t // tk, DIFF_ACC_ROWS, tk), _BF16),
            pltpu.VMEM((DIFF_ACC_ROWS, tq), _F32),
            pltpu.VMEM((DIFF_ACC_ROWS, tq), _F32),
        ],
        compiler_params=pltpu.CompilerParams(
            dimension_semantics=("parallel", "parallel", "arbitrary"), vmem_limit_bytes=limit),
        name="diff_attn",
    )(proj, proj, proj, lam_q1.reshape(1, DIFF_DK), lam_k1.reshape(1, DIFF_DK),
      lam_q2.reshape(1, DIFF_DK), lam_k2.reshape(1, DIFF_DK), g_subln.reshape(DIFF_DV, 1))


def _out_proj_kernel(x_ref, na_ref, df_ref, w_ref, o_ref):
    n_na = na_ref.shape[1]
    acc = jnp.dot(na_ref[...], w_ref[:n_na, :], preferred_element_type=_F32)
    acc += jnp.dot(df_ref[...], w_ref[n_na:, :], preferred_element_type=_F32)
    o_ref[...] = x_ref[...] + acc


def _out_proj(x, na_o, df_o, w_out, *, tm, tn):
    m, d = x.shape
    k_na, k_df = na_o.shape[1], df_o.shape[1]
    assert m % tm == 0 and d % tn == 0 and w_out.shape == (k_na + k_df, d)
    limit = _vmem_limit(2 * _nbytes((tm, tn), _F32), _nbytes((tm, k_na + k_df), _BF16),
                        _nbytes((k_na + k_df, tn), _BF16))
    return pl.pallas_call(
        _out_proj_kernel,
        grid=(m // tm, d // tn),
        in_specs=[
            pl.BlockSpec((tm, tn), lambda i, j: (i, j)),
            pl.BlockSpec((tm, k_na), lambda i, j: (i, 0)),
            pl.BlockSpec((tm, k_df), lambda i, j: (i, 0)),
            pl.BlockSpec((k_na + k_df, tn), lambda i, j: (0, j)),
        ],
        out_specs=pl.BlockSpec((tm, tn), lambda i, j: (i, j)),
        out_shape=jax.ShapeDtypeStruct((m, d), _F32),
        compiler_params=pltpu.CompilerParams(
            dimension_semantics=("parallel", "parallel"), vmem_limit_bytes=limit),
        name="out_proj",
    )(x, na_o, df_o, w_out)


def _mem_attn_kernel(q_ref, kv_ref, h_ref, w_ref, o_ref, ctx_ref):
    d = q_ref.shape[1]
    hd = d // MEM_HEADS
    for h in range(MEM_HEADS):
        qh = q_ref[:, h * hd:(h + 1) * hd]
        kh = kv_ref[:, h * hd:(h + 1) * hd]
        vh = kv_ref[:, d + h * hd:d + (h + 1) * hd]
        s = lax.dot_general(qh, kh, _NT, preferred_element_type=_F32) * (hd ** -0.5)
        p = jnp.exp(s - jnp.max(s, axis=-1, keepdims=True))
        l = jnp.sum(p, axis=-1, keepdims=True)
        ctx = jnp.dot(p.astype(_BF16), vh, preferred_element_type=_F32) / l
        ctx_ref[:, h * hd:(h + 1) * hd] = ctx.astype(ctx_ref.dtype)
    o_ref[...] = h_ref[...] + jnp.dot(ctx_ref[...], w_ref[...], preferred_element_type=_F32)


def _mem_attn(q, kv, h1, w_mo, *, seq_len, mem_tokens, tm):
    m, d = h1.shape
    assert m % tm == 0 and seq_len % tm == 0
    tiles_per_seq = seq_len // tm
    limit = _vmem_limit(_nbytes((tm, d), _BF16), _nbytes((mem_tokens, 2 * d), _BF16),
                        2 * _nbytes((tm, d), _F32), _nbytes((d, d), _BF16),
                        _nbytes((tm, d), _BF16) // 2)
    return pl.pallas_call(
        _mem_attn_kernel,
        grid=(m // tm,),
        in_specs=[
            pl.BlockSpec((tm, d), lambda i: (i, 0)),
            pl.BlockSpec((mem_tokens, 2 * d), lambda i: (i // tiles_per_seq, 0)),
            pl.BlockSpec((tm, d), lambda i: (i, 0)),
            pl.BlockSpec((d, d), lambda i: (0, 0)),
        ],
        out_specs=pl.BlockSpec((tm, d), lambda i: (i, 0)),
        out_shape=jax.ShapeDtypeStruct((m, d), _F32),
        scratch_shapes=[pltpu.VMEM((tm, d), _BF16)],
        compiler_params=pltpu.CompilerParams(
            dimension_semantics=("parallel",), vmem_limit_bytes=limit),
        name="mem_attn",
    )(q, kv, h1, w_mo)


def _ffn_kernel(h_ref, g_ref, wg_ref, wu_ref, wd_ref, gf_ref, o_ref, hn_ref):
    f = pl.program_id(1)

    def down_projection(hn):
        gate = jnp.dot(hn, wg_ref[...], preferred_element_type=_F32)
        up = jnp.dot(hn, wu_ref[...], preferred_element_type=_F32)
        act = (gate * jax.nn.sigmoid(gate)) * up
        return jnp.dot(act.astype(_BF16), wd_ref[...], preferred_element_type=_F32)

    @pl.when(f == 0)
    def _first():
        x = h_ref[...]
        inv = lax.rsqrt(jnp.mean(x * x, axis=-1, keepdims=True) + RMS_EPS)
        hn = ((x * inv) * g_ref[...]).astype(hn_ref.dtype)
        hn_ref[...] = hn
        o_ref[...] = x + down_projection(hn)

    @pl.when(f > 0)
    def _accumulate():
        o_ref[...] += down_projection(hn_ref[...])

    @pl.when(f == pl.num_programs(1) - 1)
    def _finish():
        y = o_ref[...]
        inv = lax.rsqrt(jnp.mean(y * y, axis=-1, keepdims=True) + RMS_EPS)
        o_ref[...] = (y * inv) * gf_ref[...]


def _ffn(h2, g_ffn, w_gate_up, w_down, g_final, *, tm, tf):
    m, d = h2.shape
    d_ff = w_down.shape[0]
    assert m % tm == 0 and d_ff % tf == 0 and w_gate_up.shape == (d, 2 * d_ff)
    n_f = d_ff // tf
    limit = _vmem_limit(2 * _nbytes((tm, d), _F32), 3 * _nbytes((d, tf), _BF16),
                        _nbytes((tm, d), _BF16) // 2)
    return pl.pallas_call(
        _ffn_kernel,
        grid=(m // tm, n_f),
        in_specs=[
            pl.BlockSpec((tm, d), lambda i, f: (i, 0)),
            pl.BlockSpec((1, d), lambda i, f: (0, 0)),
            pl.BlockSpec((d, tf), lambda i, f: (0, f)),
            pl.BlockSpec((d, tf), lambda i, f: (0, n_f + f)),
            pl.BlockSpec((tf, d), lambda i, f: (f, 0)),
            pl.BlockSpec((1, d), lambda i, f: (0, 0)),
        ],
        out_specs=pl.BlockSpec((tm, d), lambda i, f: (i, 0)),
        out_shape=jax.ShapeDtypeStruct((m, d), _F32),
        scratch_shapes=[pltpu.VMEM((tm, d), _BF16)],
        compiler_params=pltpu.CompilerParams(
            dimension_semantics=("parallel", "arbitrary"), vmem_limit_bytes=limit),
        name="ffn",
    )(h2, g_ffn.reshape(1, d), w_gate_up, w_gate_up, w_down, g_final.reshape(1, d))


def _trunk(x, mem, p, na_bias):
    b, t, d = x.shape
    mem_tokens = mem.shape[1]
    m = b * t
    x2 = x.reshape(m, d)
    na_w = NA_HEADS * NA_HD
    qk_w = DIFF_HEADS * 2 * DIFF_DK
    tn_in = 512
    rope_lo = 3 * na_w // tn_in
    rope = (_rope_tables(t), (rope_lo, rope_lo + qk_w // tn_in, rope_lo + 2 * qk_w // tn_in))
    proj = _norm_matmul(x2, p["g_mix"], p["w_in"], tm=1024, tn=tn_in, seq_len=t, rope=rope,
                        name="in_proj")
    proj = proj.reshape(b, t, -1)
    blk = lambda col: col // LANES
    na_o = _na_attn(proj, na_bias, col_q=0, col_k=blk(na_w), col_v=blk(2 * na_w))
    df_o = _diff_attn(proj, p["lam_q1"], p["lam_k1"], p["lam_q2"], p["lam_k2"], p["g_subln"],
                      col_q=blk(3 * na_w), col_k=blk(3 * na_w + qk_w),
                      col_v=blk(3 * na_w + 2 * qk_w), tq=512, tk=512)
    h1 = _out_proj(x2, na_o.reshape(m, -1), df_o.reshape(m, -1), p["w_out"], tm=512, tn=d)
    q_mem = _norm_matmul(h1, p["g_xattn"], p["w_mq"], tm=512, tn=d, name="mem_q")
    kv_mem = _norm_matmul(mem.reshape(b * mem_tokens, d), p["g_mem"], p["w_mkv"],
                          tm=min(512, b * mem_tokens), tn=512, name="mem_kv")
    h2 = _mem_attn(q_mem, kv_mem, h1, p["w_mo"], seq_len=t, mem_tokens=mem_tokens, tm=512)
    y = _ffn(h2, p["g_ffn"], p["w_gate_up"], p["w_down"], p["g_final"], tm=1024, tf=512)
    return y.reshape(b, t, d)


def kernel(x_prompt, x_sample, mem_prompt, mem_sample, g_mix, w_in, rpb, lam_q1, lam_k1,
           lam_q2, lam_k2, g_subln, w_out, g_xattn, g_mem, w_mq, w_mkv, w_mo, g_ffn,
           w_gate_up, w_down, g_final):
    assert w_in.shape[0] == 1, "single-layer trunk"
    p = dict(
        g_mix=g_mix[0], w_in=w_in[0].astype(_BF16), lam_q1=lam_q1[0], lam_k1=lam_k1[0],
        lam_q2=lam_q2[0], lam_k2=lam_k2[0], g_subln=g_subln[0], w_out=w_out[0].astype(_BF16),
        g_xattn=g_xattn[0], g_mem=g_mem[0], w_mq=w_mq[0].astype(_BF16),
        w_mkv=w_mkv[0].astype(_BF16), w_mo=w_mo[0].astype(_BF16), g_ffn=g_ffn[0],
        w_gate_up=w_gate_up[0].astype(_BF16), w_down=w_down[0].astype(_BF16), g_final=g_final,
    )
    na_bias = _na_bias(rpb[0])
    return (_trunk(x_prompt, mem_prompt, p, na_bias), _trunk(x_sample, mem_sample, p, na_bias))
```

```python
import functools
import math

import jax
import jax.numpy as jnp
import numpy as np
from jax import lax
from jax.experimental import pallas as pl
from jax.experimental.pallas import tpu as pltpu

GRID_W = 64
NA_HEADS = 8
NA_HD = 128
NA_WIN_ROWS = 8
NA_WIN_COLS = 16
DIFF_HEADS = 8
DIFF_DK = 64
DIFF_DV = 2 * DIFF_DK
ROPE_THETA = 10000.0
MEM_HEADS = 4
RMS_EPS = 1e-6
SUBLN_EPS = 1e-5
LAM_INIT = 0.8 - 0.6 * math.exp(-0.3 * 0)
DIFF_Q_SCALE = DIFF_DK ** -0.5 * math.log2(math.e)
BF16_SUBLANES = 16
DIFF_ACC_ROWS = DIFF_DV + BF16_SUBLANES
DIFF_CHUNKS_PER_TRIP = 16
DIFF_MAX_LAG = 64.0

LANES = 128
MXU_COLS = 256
V7X_VMEM_BYTES = 64 * 1024 * 1024
VMEM_LIMIT_CAP = V7X_VMEM_BYTES - 6 * 1024 * 1024
INTERNAL_SCRATCH_ALLOWANCE = 12 * 1024 * 1024

NA_Q_ROWS = 4
NA_K_ROWS = NA_Q_ROWS + NA_WIN_ROWS
NA_TQ = NA_Q_ROWS * GRID_W
NA_TK = NA_K_ROWS * GRID_W
NA_BLOCKS_PER_STEP = 8
ROPE_ROW_GROUP = 256
MASK_VALUE = -1e30

_NT = (((1,), (1,)), ((), ()))

_BF16 = jnp.bfloat16
_F32 = jnp.float32


def _vmem_limit(*block_bytes):
    return int(min(VMEM_LIMIT_CAP, 2 * sum(block_bytes) + INTERNAL_SCRATCH_ALLOWANCE))


def _nbytes(shape, dtype):
    return int(np.prod(shape)) * jnp.dtype(dtype).itemsize


def _rms_normalize(x_ref, g_ref):
    x = x_ref[...]
    inv = lax.rsqrt(jnp.mean(x * x, axis=-1, keepdims=True) + RMS_EPS)
    return ((x * inv) * g_ref[...]).astype(_BF16)


def _norm_matmul_kernel(x_ref, g_ref, w_ref, o_ref, *scratch):
    if not scratch:
        acc = jnp.dot(_rms_normalize(x_ref, g_ref), w_ref[...], preferred_element_type=_F32)
        o_ref[...] = acc.astype(o_ref.dtype)
        return

    (xn_ref,) = scratch
    j = pl.program_id(1)

    @pl.when(j == 0)
    def _first():
        xn = _rms_normalize(x_ref, g_ref)
        xn_ref[...] = xn
        o_ref[...] = jnp.dot(xn, w_ref[...], preferred_element_type=_F32).astype(o_ref.dtype)

    @pl.when(j > 0)
    def _rest():
        acc = jnp.dot(xn_ref[...], w_ref[...], preferred_element_type=_F32)
        o_ref[...] = acc.astype(o_ref.dtype)


def _norm_matmul(x, g, w, *, tm, tn, name):
    m, d = x.shape
    n = w.shape[1]
    assert m % tm == 0 and n % tn == 0
    limit = _vmem_limit(_nbytes((tm, d), _F32), _nbytes((d, tn), _BF16), _nbytes((tm, tn), _BF16),
                        _nbytes((tm, d), _BF16) // 2, _nbytes((tm, tn), _F32))
    return pl.pallas_call(
        _norm_matmul_kernel,
        grid=(m // tm, n // tn),
        in_specs=[
            pl.BlockSpec((tm, d), lambda i, j: (i, 0)),
            pl.BlockSpec((1, d), lambda i, j: (0, 0)),
            pl.BlockSpec((d, tn), lambda i, j: (0, j)),
        ],
        out_specs=pl.BlockSpec((tm, tn), lambda i, j: (i, j)),
        out_shape=jax.ShapeDtypeStruct((m, n), _BF16),
        scratch_shapes=[] if n == tn else [pltpu.VMEM((tm, d), _BF16)],
        compiler_params=pltpu.CompilerParams(
            dimension_semantics=("parallel", "arbitrary"), vmem_limit_bytes=limit),
        name=name,
    )(x, g.reshape(1, d), w)


def _in_proj_kernel(x_ref, g_ref, w_ref, cos_ref, sin_up_ref, sin_dn_ref, o_ref, xn_ref, *,
                    q_cols, k_cols):
    xn_ref[...] = _rms_normalize(x_ref, g_ref)
    tm, n = o_ref.shape
    rows = min(tm, ROPE_ROW_GROUP)
    for n0 in range(0, n, MXU_COLS):
        cols = slice(n0, n0 + MXU_COLS)
        is_q = q_cols[0] <= n0 < q_cols[1]
        if not (is_q or k_cols[0] <= n0 < k_cols[1]):
            acc = jnp.dot(xn_ref[...], w_ref[:, cols], preferred_element_type=_F32)
            o_ref[:, cols] = acc.astype(o_ref.dtype)
            continue
        for r0 in range(0, tm, rows):
            rs = slice(r0, r0 + rows)
            cos, sin_up, sin_dn = cos_ref[rs, :], sin_up_ref[rs, :], sin_dn_ref[rs, :]
            acc = jnp.dot(xn_ref[rs, :], w_ref[:, cols], preferred_element_type=_F32)
            for c0 in range(0, MXU_COLS, LANES):
                a = acc[:, c0:c0 + LANES]
                r = (a * cos + pltpu.roll(a, LANES - DIFF_DK // 2, 1) * sin_up
                     + pltpu.roll(a, DIFF_DK // 2, 1) * sin_dn)
                if is_q:
                    r = r * DIFF_Q_SCALE
                o_ref[rs, n0 + c0:n0 + c0 + LANES] = r.astype(o_ref.dtype)


def _in_proj(x, g, w, tables, *, seq_len, tm, q_cols, k_cols):
    m, d = x.shape
    n = w.shape[1]
    assert m % tm == 0 and seq_len % tm == 0 and n % MXU_COLS == 0
    assert all(c % MXU_COLS == 0 for c in q_cols + k_cols)
    tiles_per_seq = seq_len // tm
    table_spec = pl.BlockSpec((tm, LANES), lambda i: (i % tiles_per_seq, 0))
    limit = _vmem_limit(_nbytes((tm, d), _F32), _nbytes((d, n), _BF16) // 2,
                        _nbytes((tm, n), _BF16), 3 * _nbytes((tm, LANES), _F32),
                        _nbytes((tm, d), _BF16) // 2)
    return pl.pallas_call(
        functools.partial(_in_proj_kernel, q_cols=q_cols, k_cols=k_cols),
        grid=(m // tm,),
        in_specs=[
            pl.BlockSpec((tm, d), lambda i: (i, 0)),
            pl.BlockSpec((1, d), lambda i: (0, 0)),
            pl.BlockSpec((d, n), lambda i: (0, 0), pipeline_mode=pl.Buffered(1)),
            table_spec, table_spec, table_spec,
        ],
        out_specs=pl.BlockSpec((tm, n), lambda i: (i, 0)),
        out_shape=jax.ShapeDtypeStruct((m, n), _BF16),
        scratch_shapes=[pltpu.VMEM((tm, d), _BF16)],
        compiler_params=pltpu.CompilerParams(
            dimension_semantics=("parallel",), vmem_limit_bytes=limit),
        name="in_proj",
    )(x, g.reshape(1, d), w, *tables)


def _rope_tables(seq_len):
    inv = 1.0 / (ROPE_THETA ** (jnp.arange(0, DIFF_DK, 2, dtype=_F32) / DIFF_DK))
    ang = jnp.arange(seq_len, dtype=_F32)[:, None] * inv[None, :]
    ang = jnp.tile(ang, (1, LANES // (DIFF_DK // 2)))
    cos, sin = jnp.cos(ang), jnp.sin(ang)
    first_half = jnp.asarray((np.arange(LANES) % DIFF_DK) < DIFF_DK // 2)[None, :]
    sin_up = jnp.where(first_half, -sin, 0.0)
    sin_dn = jnp.where(first_half, 0.0, sin)
    return cos, sin_up, sin_dn


def _na_window(variant, i):
    if variant == 0:
        return 0, NA_WIN_ROWS - 1 - i
    if variant == 1:
        return i, NA_WIN_ROWS // 2 - 1 - i
    return NA_K_ROWS - NA_WIN_ROWS, -1 - i


def _na_bias_kernel(rpb_ref, o_ref):
    h = pl.program_id(0)
    n_dc = 2 * NA_WIN_COLS - 1
    c = lax.broadcasted_iota(jnp.int32, (GRID_W, LANES), 0)
    lane = lax.broadcasted_iota(jnp.int32, (GRID_W, LANES), 1)
    kc = lane & (GRID_W - 1)
    cs = jnp.clip(c - NA_WIN_COLS // 2, 0, GRID_W - NA_WIN_COLS)
    col_ok = jnp.logical_and(kc >= cs, kc < cs + NA_WIN_COLS)
    dc = kc - c + (NA_WIN_COLS - 1)
    neg = jnp.full((GRID_W, LANES), MASK_VALUE, _F32)
    by_dr = []
    for d in range(2 * NA_WIN_ROWS - 1):
        u = neg
        for n in range(n_dc):
            u = jnp.where(dc == n, rpb_ref[h, d * n_dc + n], u)
        by_dr.append(jnp.where(col_ok, u, neg))
    low_half = lane < GRID_W
    for variant in range(3):
        for i in range(NA_Q_ROWS):
            first, dr0 = _na_window(variant, i)
            for t in range(NA_K_ROWS // 2):
                halves = []
                for jj in (2 * t, 2 * t + 1):
                    valid = first <= jj < first + NA_WIN_ROWS
                    halves.append(by_dr[dr0 + jj] if valid else neg)
                tile = (halves[0] if halves[0] is halves[1]
                        else jnp.where(low_half, halves[0], halves[1]))
                o_ref[variant, i * GRID_W:(i + 1) * GRID_W, t * LANES:(t + 1) * LANES] = tile


def _na_bias(rpb):
    n_tab = (2 * NA_WIN_ROWS - 1) * (2 * NA_WIN_COLS - 1)
    return pl.pallas_call(
        _na_bias_kernel,
        grid=(NA_HEADS,),
        in_specs=[pl.BlockSpec(memory_space=pltpu.SMEM)],
        out_specs=pl.BlockSpec((None, 3, NA_TQ, NA_TK), lambda h: (h, 0, 0, 0)),
        out_shape=jax.ShapeDtypeStruct((NA_HEADS, 3, NA_TQ, NA_TK), _F32),
        compiler_params=pltpu.CompilerParams(dimension_semantics=("parallel",)),
        name="na_bias",
    )(rpb.reshape(NA_HEADS, n_tab))


def _na_attn_kernel(q_ref, k_ref, v_ref, bias_ref, o_ref, *, rows):
    last = pl.num_programs(2) * NA_BLOCKS_PER_STEP - 1
    for u in range(NA_BLOCKS_PER_STEP):
        r = pl.program_id(2) * NA_BLOCKS_PER_STEP + u
        kr0 = jnp.clip(r * NA_Q_ROWS - NA_WIN_ROWS // 2, 0, rows - NA_K_ROWS)
        start = pl.multiple_of(kr0 * GRID_W, GRID_W)
        kb = k_ref[pl.ds(start, NA_TK), :]
        vb = v_ref[pl.ds(start, NA_TK), :]
        variant = jnp.where(r == 0, 0, jnp.where(r == last, 2, 1))
        q = q_ref[u * NA_TQ:(u + 1) * NA_TQ, :]
        s = lax.dot_general(q, kb, _NT, preferred_element_type=_F32)
        s = s * (NA_HD ** -0.5) + bias_ref[variant]
        p = jnp.exp(s - jnp.max(s, axis=-1, keepdims=True))
        l = jnp.sum(p, axis=-1, keepdims=True)
        o = jnp.dot(p.astype(_BF16), vb, preferred_element_type=_F32) / l
        o_ref[u * NA_TQ:(u + 1) * NA_TQ, :] = o.astype(o_ref.dtype)


def _na_attn(proj, bias, *, col_q, col_k, col_v):
    b, t, _ = proj.shape
    rows = t // GRID_W
    tq = NA_BLOCKS_PER_STEP * NA_TQ
    assert t % tq == 0 and rows >= NA_K_ROWS + NA_Q_ROWS
    limit = _vmem_limit(2 * _nbytes((t, NA_HD), _BF16), _nbytes((3, NA_TQ, NA_TK), _F32),
                        2 * _nbytes((tq, NA_HD), _BF16),
                        2 * NA_BLOCKS_PER_STEP * _nbytes((NA_TQ, NA_TK), _F32))
    return pl.pallas_call(
        functools.partial(_na_attn_kernel, rows=rows),
        grid=(b, NA_HEADS, t // tq),
        in_specs=[
            pl.BlockSpec((None, tq, NA_HD), lambda bi, h, r: (bi, r, col_q + h)),
            pl.BlockSpec((None, t, NA_HD), lambda bi, h, r: (bi, 0, col_k + h)),
            pl.BlockSpec((None, t, NA_HD), lambda bi, h, r: (bi, 0, col_v + h)),
            pl.BlockSpec((None, 3, NA_TQ, NA_TK), lambda bi, h, r: (h, 0, 0, 0)),
        ],
        out_specs=pl.BlockSpec((None, tq, NA_HD), lambda bi, h, r: (bi, r, h)),
        out_shape=jax.ShapeDtypeStruct((b, t, NA_HEADS * NA_HD), _BF16),
        compiler_params=pltpu.CompilerParams(
            dimension_semantics=("parallel", "parallel", "arbitrary"), vmem_limit_bytes=limit),
        name="na_attn",
    )(proj, proj, proj, bias)


def _diff_attn_kernel(q_ref, k_ref, v_ref, lq1_ref, lk1_ref, lq2_ref, lk2_ref, g_ref, o_ref,
                      vt_ref, acc1_ref, acc2_ref, *, tk, chunks_per_trip):
    n_chunks = vt_ref.shape[0]

    @pl.when(pl.program_id(2) == 0)
    def _transpose_values():
        row = lax.broadcasted_iota(jnp.int32, (BF16_SUBLANES, tk), 0)
        ones_tile = jnp.where(row == 0, 1.0, 0.0).astype(vt_ref.dtype)

        def body(c, carry):
            off = pl.multiple_of(c * tk, tk)
            vt_ref[c, :DIFF_DV, :] = v_ref[pl.ds(off, tk), :].astype(_F32).T.astype(vt_ref.dtype)
            vt_ref[c, DIFF_DV:, :] = ones_tile
            return carry
        lax.fori_loop(0, n_chunks, body, 0)

    q = q_ref[...]
    tq = q.shape[0]
    lane = lax.broadcasted_iota(jnp.int32, q.shape, 1)
    zero = jnp.zeros_like(q)
    qa = jnp.where(lane < DIFF_DK, q, zero)
    qb = jnp.where(lane >= DIFF_DK, q, zero)
    acc_refs = (acc1_ref, acc2_ref)

    def chunk_scores(c):
        off = pl.multiple_of(c * tk, tk)
        kc = k_ref[pl.ds(off, tk), :]
        return [lax.dot_general(kc, qh, _NT, preferred_element_type=_F32) for qh in (qa, qb)]

    def weighted_values(c, s, m):
        return jnp.dot(vt_ref[c], jnp.exp2(s - m).astype(_BF16), preferred_element_type=_F32)

    def first_chunk():
        m = []
        for s, acc_ref in zip(chunk_scores(0), acc_refs):
            m.append(jnp.max(s, axis=0, keepdims=True))
            acc_ref[...] = weighted_values(0, s, m[-1])
        return tuple(m)

    def lazy_chunk(c, state):
        new_state = []
        for s, acc_ref, (top, lag) in zip(chunk_scores(c), acc_refs, state):
            m = jnp.maximum(top, 0.0)
            pv = weighted_values(c, s, m)
            chunk_max = jnp.max(s, axis=0, keepdims=True)
            acc_ref[...] = (acc_ref[...] + pv) * jnp.exp2(m - jnp.maximum(m, chunk_max))
            new_state.append((jnp.maximum(top, chunk_max), jnp.maximum(lag, chunk_max - m)))
        return tuple(new_state)

    def max_first_chunk(c, m_old):
        m_out = []
        for s, acc_ref, m in zip(chunk_scores(c), acc_refs, m_old):
            m_new = jnp.maximum(m, jnp.max(s, axis=0, keepdims=True))
            acc_ref[...] = jnp.exp2(m - m_new) * acc_ref[...] + weighted_values(c, s, m_new)
            m_out.append(m_new)
        return tuple(m_out)

    n_trips, n_tail = divmod(n_chunks, chunks_per_trip)

    def fast_body(j, state):
        for u in range(chunks_per_trip):
            state = lazy_chunk(j * chunks_per_trip + u, state)
        return state

    for acc_ref in acc_refs:
        acc_ref[...] = jnp.zeros_like(acc_ref)
    start = (jnp.full((1, tq), MASK_VALUE, _F32), jnp.zeros((1, tq), _F32))
    state = lax.fori_loop(0, n_trips, fast_body, (start, start))
    for c in range(n_chunks - n_tail, n_chunks):
        state = lazy_chunk(c, state)
    worst_lag = jnp.max(jnp.maximum(jnp.maximum(state[0][1], -state[0][0]),
                                    jnp.maximum(state[1][1], -state[1][0])))

    @pl.when(jnp.logical_not(worst_lag <= DIFF_MAX_LAG))
    def _slow_path():
        lax.fori_loop(1, n_chunks, max_first_chunk, first_chunk())

    lam = (jnp.exp(jnp.sum(lq1_ref[...] * lk1_ref[...], axis=-1, keepdims=True))
           - jnp.exp(jnp.sum(lq2_ref[...] * lk2_ref[...], axis=-1, keepdims=True)) + LAM_INIT)
    o1 = acc1_ref[:DIFF_DV, :] / acc1_ref[DIFF_DV:DIFF_DV + 1, :]
    o2 = acc2_ref[:DIFF_DV, :] / acc2_ref[DIFF_DV:DIFF_DV + 1, :]
    o = o1 - lam * o2
    inv = lax.rsqrt(jnp.mean(o * o, axis=0, keepdims=True) + SUBLN_EPS)
    o = ((o * inv) * g_ref[...]) * (1.0 - LAM_INIT)
    o_ref[...] = o.T.astype(o_ref.dtype)


def _diff_attn(proj, lam_q1, lam_k1, lam_q2, lam_k2, g_subln, *, col_q, col_k, col_v, tq, tk):
    b, t, _ = proj.shape
    assert t % tq == 0 and t % tk == 0 and t // tk >= 2
    lam_spec = pl.BlockSpec((1, DIFF_DK), lambda bi, h, i: (0, 0))
    limit = _vmem_limit(2 * _nbytes((t, LANES), _BF16), 2 * _nbytes((tq, LANES), _BF16),
                        _nbytes((t, LANES), _BF16) // 2, _nbytes((DIFF_DV, tq), _F32),
                        2 * _nbytes((tk, tq), _F32))
    return pl.pallas_call(
        functools.partial(_diff_attn_kernel, tk=tk, chunks_per_trip=DIFF_CHUNKS_PER_TRIP),
        grid=(b, DIFF_HEADS, t // tq),
        in_specs=[
            pl.BlockSpec((None, tq, LANES), lambda bi, h, i: (bi, i, col_q + h)),
            pl.BlockSpec((None, t, LANES), lambda bi, h, i: (bi, 0, col_k + h)),
            pl.BlockSpec((None, t, LANES), lambda bi, h, i: (bi, 0, col_v + h)),
            lam_spec, lam_spec, lam_spec, lam_spec,
            pl.BlockSpec((DIFF_DV, 1), lambda bi, h, i: (0, 0)),
        ],
        out_specs=pl.BlockSpec((None, tq, DIFF_DV), lambda bi, h, i: (bi, i, h)),
        out_shape=jax.ShapeDtypeStruct((b, t, DIFF_HEADS * DIFF_DV), _BF16),
        scratch_shapes=[
            pltpu.VMEM((t // tk, DIFF_ACC_ROWS, tk), _BF16),
            pltpu.VMEM((DIFF_ACC_ROWS, tq), _F32),
            pltpu.VMEM((DIFF_ACC_ROWS, tq), _F32),
        ],
        compiler_params=pltpu.CompilerParams(
            dimension_semantics=("parallel", "parallel", "arbitrary"), vmem_limit_bytes=limit),
        name="diff_attn",
    )(proj, proj, proj, lam_q1.reshape(1, DIFF_DK), lam_k1.reshape(1, DIFF_DK),
      lam_q2.reshape(1, DIFF_DK), lam_k2.reshape(1, DIFF_DK), g_subln.reshape(DIFF_DV, 1))


def _out_proj_kernel(x_ref, na_ref, df_ref, w_ref, o_ref):
    n_na = na_ref.shape[1]
    acc = jnp.dot(na_ref[...], w_ref[:n_na, :], preferred_element_type=_F32)
    acc += jnp.dot(df_ref[...], w_ref[n_na:, :], preferred_element_type=_F32)
    o_ref[...] = x_ref[...] + acc


def _out_proj(x, na_o, df_o, w_out, *, tm, tn):
    m, d = x.shape
    k_na, k_df = na_o.shape[1], df_o.shape[1]
    assert m % tm == 0 and d % tn == 0 and w_out.shape == (k_na + k_df, d)
    limit = _vmem_limit(2 * _nbytes((tm, tn), _F32), _nbytes((tm, k_na + k_df), _BF16),
                        _nbytes((k_na + k_df, tn), _BF16))
    return pl.pallas_call(
        _out_proj_kernel,
        grid=(m // tm, d // tn),
        in_specs=[
            pl.BlockSpec((tm, tn), lambda i, j: (i, j)),
            pl.BlockSpec((tm, k_na), lambda i, j: (i, 0)),
            pl.BlockSpec((tm, k_df), lambda i, j: (i, 0)),
            pl.BlockSpec((k_na + k_df, tn), lambda i, j: (0, j)),
        ],
        out_specs=pl.BlockSpec((tm, tn), lambda i, j: (i, j)),
        out_shape=jax.ShapeDtypeStruct((m, d), _F32),
        compiler_params=pltpu.CompilerParams(
            dimension_semantics=("parallel", "parallel"), vmem_limit_bytes=limit),
        name="out_proj",
    )(x, na_o, df_o, w_out)


def _mem_attn_kernel(q_ref, kv_ref, h_ref, w_ref, o_ref, ctx_ref):
    d = q_ref.shape[1]
    hd = d // MEM_HEADS
    for h in range(MEM_HEADS):
        qh = q_ref[:, h * hd:(h + 1) * hd]
        kh = kv_ref[:, h * hd:(h + 1) * hd]
        vh = kv_ref[:, d + h * hd:d + (h + 1) * hd]
        s = lax.dot_general(qh, kh, _NT, preferred_element_type=_F32) * (hd ** -0.5)
        p = jnp.exp(s - jnp.max(s, axis=-1, keepdims=True))
        l = jnp.sum(p, axis=-1, keepdims=True)
        ctx = jnp.dot(p.astype(_BF16), vh, preferred_element_type=_F32) / l
        ctx_ref[:, h * hd:(h + 1) * hd] = ctx.astype(ctx_ref.dtype)
    o_ref[...] = h_ref[...] + jnp.dot(ctx_ref[...], w_ref[...], preferred_element_type=_F32)


def _mem_attn(q, kv, h1, w_mo, *, seq_len, mem_tokens, tm):
    m, d = h1.shape
    assert m % tm == 0 and seq_len % tm == 0
    tiles_per_seq = seq_len // tm
    limit = _vmem_limit(_nbytes((tm, d), _BF16), _nbytes((mem_tokens, 2 * d), _BF16),
                        2 * _nbytes((tm, d), _F32), _nbytes((d, d), _BF16),
                        _nbytes((tm, d), _BF16) // 2)
    return pl.pallas_call(
        _mem_attn_kernel,
        grid=(m // tm,),
        in_specs=[
            pl.BlockSpec((tm, d), lambda i: (i, 0)),
            pl.BlockSpec((mem_tokens, 2 * d), lambda i: (i // tiles_per_seq, 0)),
            pl.BlockSpec((tm, d), lambda i: (i, 0)),
            pl.BlockSpec((d, d), lambda i: (0, 0)),
        ],
        out_specs=pl.BlockSpec((tm, d), lambda i: (i, 0)),
        out_shape=jax.ShapeDtypeStruct((m, d), _F32),
        scratch_shapes=[pltpu.VMEM((tm, d), _BF16)],
        compiler_params=pltpu.CompilerParams(
            dimension_semantics=("parallel",), vmem_limit_bytes=limit),
        name="mem_attn",
    )(q, kv, h1, w_mo)


def _ffn_kernel(h_ref, g_ref, wg_ref, wu_ref, wd_ref, gf_ref, o_ref, hn_ref):
    f = pl.program_id(1)

    def down_projection(hn):
        gate = jnp.dot(hn, wg_ref[...], preferred_element_type=_F32)
        up = jnp.dot(hn, wu_ref[...], preferred_element_type=_F32)
        act = (gate * jax.nn.sigmoid(gate)) * up
        return jnp.dot(act.astype(_BF16), wd_ref[...], preferred_element_type=_F32)

    @pl.when(f == 0)
    def _first():
        x = h_ref[...]
        inv = lax.rsqrt(jnp.mean(x * x, axis=-1, keepdims=True) + RMS_EPS)
        hn = ((x * inv) * g_ref[...]).astype(hn_ref.dtype)
        hn_ref[...] = hn
        o_ref[...] = x + down_projection(hn)

    @pl.when(f > 0)
    def _accumulate():
        o_ref[...] += down_projection(hn_ref[...])

    @pl.when(f == pl.num_programs(1) - 1)
    def _finish():
        y = o_ref[...]
        inv = lax.rsqrt(jnp.mean(y * y, axis=-1, keepdims=True) + RMS_EPS)
        o_ref[...] = (y * inv) * gf_ref[...]


def _ffn(h2, g_ffn, w_gate_up, w_down, g_final, *, tm, tf):
    m, d = h2.shape
    d_ff = w_down.shape[0]
    assert m % tm == 0 and d_ff % tf == 0 and w_gate_up.shape == (d, 2 * d_ff)
    n_f = d_ff // tf
    limit = _vmem_limit(2 * _nbytes((tm, d), _F32), 3 * _nbytes((d, tf), _BF16),
                        _nbytes((tm, d), _BF16) // 2)
    return pl.pallas_call(
        _ffn_kernel,
        grid=(m // tm, n_f),
        in_specs=[
            pl.BlockSpec((tm, d), lambda i, f: (i, 0)),
            pl.BlockSpec((1, d), lambda i, f: (0, 0)),
            pl.BlockSpec((d, tf), lambda i, f: (0, f)),
            pl.BlockSpec((d, tf), lambda i, f: (0, n_f + f)),
            pl.BlockSpec((tf, d), lambda i, f: (f, 0)),
            pl.BlockSpec((1, d), lambda i, f: (0, 0)),
        ],
        out_specs=pl.BlockSpec((tm, d), lambda i, f: (i, 0)),
        out_shape=jax.ShapeDtypeStruct((m, d), _F32),
        scratch_shapes=[pltpu.VMEM((tm, d), _BF16)],
        compiler_params=pltpu.CompilerParams(
            dimension_semantics=("parallel", "arbitrary"), vmem_limit_bytes=limit),
        name="ffn",
    )(h2, g_ffn.reshape(1, d), w_gate_up, w_gate_up, w_down, g_final.reshape(1, d))


def _trunk(x, mem, p, na_bias):
    b, t, d = x.shape
    mem_tokens = mem.shape[1]
    m = b * t
    x2 = x.reshape(m, d)
    na_w = NA_HEADS * NA_HD
    qk_w = DIFF_HEADS * 2 * DIFF_DK
    q_cols = (3 * na_w, 3 * na_w + qk_w)
    k_cols = (3 * na_w + qk_w, 3 * na_w + 2 * qk_w)
    proj = _in_proj(x2, p["g_mix"], p["w_in"], _rope_tables(t), seq_len=t, tm=512,
                    q_cols=q_cols, k_cols=k_cols)
    proj = proj.reshape(b, t, -1)
    blk = lambda col: col // LANES
    na_o = _na_attn(proj, na_bias, col_q=0, col_k=blk(na_w), col_v=blk(2 * na_w))
    df_o = _diff_attn(proj, p["lam_q1"], p["lam_k1"], p["lam_q2"], p["lam_k2"], p["g_subln"],
                      col_q=blk(3 * na_w), col_k=blk(3 * na_w + qk_w),
                      col_v=blk(3 * na_w + 2 * qk_w), tq=512, tk=512)
    h1 = _out_proj(x2, na_o.reshape(m, -1), df_o.reshape(m, -1), p["w_out"], tm=512, tn=d)
    q_mem = _norm_matmul(h1, p["g_xattn"], p["w_mq"], tm=512, tn=d, name="mem_q")
    kv_mem = _norm_matmul(mem.reshape(b * mem_tokens, d), p["g_mem"], p["w_mkv"],
                          tm=min(512, b * mem_tokens), tn=512, name="mem_kv")
    h2 = _mem_attn(q_mem, kv_mem, h1, p["w_mo"], seq_len=t, mem_tokens=mem_tokens, tm=512)
    y = _ffn(h2, p["g_ffn"], p["w_gate_up"], p["w_down"], p["g_final"], tm=1024, tf=512)
    return y.reshape(b, t, d)


def kernel(x_prompt, x_sample, mem_prompt, mem_sample, g_mix, w_in, rpb, lam_q1, lam_k1,
           lam_q2, lam_k2, g_subln, w_out, g_xattn, g_mem, w_mq, w_mkv, w_mo, g_ffn,
           w_gate_up, w_down, g_final):
    assert w_in.shape[0] == 1, "single-layer trunk"
    p = dict(
        g_mix=g_mix[0], w_in=w_in[0].astype(_BF16), lam_q1=lam_q1[0], lam_k1=lam_k1[0],
        lam_q2=lam_q2[0], lam_k2=lam_k2[0], g_subln=g_subln[0], w_out=w_out[0].astype(_BF16),
        g_xattn=g_xattn[0], g_mem=g_mem[0], w_mq=w_mq[0].astype(_BF16),
        w_mkv=w_mkv[0].astype(_BF16), w_mo=w_mo[0].astype(_BF16), g_ffn=g_ffn[0],
        w_gate_up=w_gate_up[0].astype(_BF16), w_down=w_down[0].astype(_BF16), g_final=g_final,
    )
    na_bias = _na_bias(rpb[0])
    return (_trunk(x_prompt, mem_prompt, p, na_bias), _trunk(x_sample, mem_sample, p, na_bias))
```

```python
import functools
import math

import jax
import jax.numpy as jnp
import numpy as np
from jax import lax
from jax.experimental import pallas as pl
from jax.experimental.pallas import tpu as pltpu

GRID_W = 64
NA_HEADS = 8
NA_HD = 128
NA_WIN_ROWS = 8
NA_WIN_COLS = 16
DIFF_HEADS = 8
DIFF_DK = 64
DIFF_DV = 2 * DIFF_DK
ROPE_THETA = 10000.0
MEM_HEADS = 4
RMS_EPS = 1e-6
SUBLN_EPS = 1e-5
LAM_INIT = 0.8 - 0.6 * math.exp(-0.3 * 0)
LOG2_E = math.log2(math.e)
DIFF_Q_SCALE = DIFF_DK ** -0.5 * LOG2_E
BF16_SUBLANES = 16
DIFF_ACC_ROWS = DIFF_DV + BF16_SUBLANES
DIFF_CHUNKS_PER_TRIP = 8
DIFF_MAX_LAG = 64.0

LANES = 128
MXU_COLS = 256
V7X_VMEM_BYTES = 64 * 1024 * 1024
VMEM_LIMIT_CAP = V7X_VMEM_BYTES - 6 * 1024 * 1024
INTERNAL_SCRATCH_ALLOWANCE = 12 * 1024 * 1024

NA_Q_ROWS = 4
NA_K_ROWS = NA_Q_ROWS + NA_WIN_ROWS
NA_TQ = NA_Q_ROWS * GRID_W
NA_TK = NA_K_ROWS * GRID_W
NA_BLOCKS_PER_STEP = 8
ROPE_ROW_GROUP = 256
MASK_VALUE = -1e30

_NT = (((1,), (1,)), ((), ()))

_BF16 = jnp.bfloat16
_F32 = jnp.float32


def _vmem_limit(*block_bytes):
    return int(min(VMEM_LIMIT_CAP, 2 * sum(block_bytes) + INTERNAL_SCRATCH_ALLOWANCE))


def _nbytes(shape, dtype):
    return int(np.prod(shape)) * jnp.dtype(dtype).itemsize


def _rms_normalize(x_ref, g_ref):
    x = x_ref[...]
    inv = lax.rsqrt(jnp.mean(x * x, axis=-1, keepdims=True) + RMS_EPS)
    return ((x * inv) * g_ref[...]).astype(_BF16)


def _norm_matmul_kernel(x_ref, g_ref, w_ref, o_ref, *scratch):
    if not scratch:
        acc = jnp.dot(_rms_normalize(x_ref, g_ref), w_ref[...], preferred_element_type=_F32)
        o_ref[...] = acc.astype(o_ref.dtype)
        return

    (xn_ref,) = scratch
    j = pl.program_id(1)

    @pl.when(j == 0)
    def _first():
        xn = _rms_normalize(x_ref, g_ref)
        xn_ref[...] = xn
        o_ref[...] = jnp.dot(xn, w_ref[...], preferred_element_type=_F32).astype(o_ref.dtype)

    @pl.when(j > 0)
    def _rest():
        acc = jnp.dot(xn_ref[...], w_ref[...], preferred_element_type=_F32)
        o_ref[...] = acc.astype(o_ref.dtype)


def _norm_matmul(x, g, w, *, tm, tn, name):
    m, d = x.shape
    n = w.shape[1]
    assert m % tm == 0 and n % tn == 0
    limit = _vmem_limit(_nbytes((tm, d), _F32), _nbytes((d, tn), _BF16), _nbytes((tm, tn), _BF16),
                        _nbytes((tm, d), _BF16) // 2, _nbytes((tm, tn), _F32))
    return pl.pallas_call(
        _norm_matmul_kernel,
        grid=(m // tm, n // tn),
        in_specs=[
            pl.BlockSpec((tm, d), lambda i, j: (i, 0)),
            pl.BlockSpec((1, d), lambda i, j: (0, 0)),
            pl.BlockSpec((d, tn), lambda i, j: (0, j)),
        ],
        out_specs=pl.BlockSpec((tm, tn), lambda i, j: (i, j)),
        out_shape=jax.ShapeDtypeStruct((m, n), _BF16),
        scratch_shapes=[] if n == tn else [pltpu.VMEM((tm, d), _BF16)],
        compiler_params=pltpu.CompilerParams(
            dimension_semantics=("parallel", "arbitrary"), vmem_limit_bytes=limit),
        name=name,
    )(x, g.reshape(1, d), w)


def _in_proj_kernel(x_ref, g_ref, w_ref, cos_ref, sin_up_ref, sin_dn_ref, o_ref, xn_ref, *,
                    q_cols, k_cols):
    xn_ref[...] = _rms_normalize(x_ref, g_ref)
    tm, n = o_ref.shape
    rows = min(tm, ROPE_ROW_GROUP)
    for n0 in range(0, n, MXU_COLS):
        cols = slice(n0, n0 + MXU_COLS)
        is_q = q_cols[0] <= n0 < q_cols[1]
        if not (is_q or k_cols[0] <= n0 < k_cols[1]):
            acc = jnp.dot(xn_ref[...], w_ref[:, cols], preferred_element_type=_F32)
            o_ref[:, cols] = acc.astype(o_ref.dtype)
            continue
        for r0 in range(0, tm, rows):
            rs = slice(r0, r0 + rows)
            cos, sin_up, sin_dn = cos_ref[rs, :], sin_up_ref[rs, :], sin_dn_ref[rs, :]
            acc = jnp.dot(xn_ref[rs, :], w_ref[:, cols], preferred_element_type=_F32)
            for c0 in range(0, MXU_COLS, LANES):
                a = acc[:, c0:c0 + LANES]
                r = (a * cos + pltpu.roll(a, LANES - DIFF_DK // 2, 1) * sin_up
                     + pltpu.roll(a, DIFF_DK // 2, 1) * sin_dn)
                if is_q:
                    r = r * DIFF_Q_SCALE
                o_ref[rs, n0 + c0:n0 + c0 + LANES] = r.astype(o_ref.dtype)


def _in_proj(x, g, w, tables, *, seq_len, tm, q_cols, k_cols):
    m, d = x.shape
    n = w.shape[1]
    assert m % tm == 0 and seq_len % tm == 0 and n % MXU_COLS == 0
    assert all(c % MXU_COLS == 0 for c in q_cols + k_cols)
    tiles_per_seq = seq_len // tm
    table_spec = pl.BlockSpec((tm, LANES), lambda i: (i % tiles_per_seq, 0))
    limit = _vmem_limit(_nbytes((tm, d), _F32), _nbytes((d, n), _BF16) // 2,
                        _nbytes((tm, n), _BF16), 3 * _nbytes((tm, LANES), _F32),
                        _nbytes((tm, d), _BF16) // 2)
    return pl.pallas_call(
        functools.partial(_in_proj_kernel, q_cols=q_cols, k_cols=k_cols),
        grid=(m // tm,),
        in_specs=[
            pl.BlockSpec((tm, d), lambda i: (i, 0)),
            pl.BlockSpec((1, d), lambda i: (0, 0)),
            pl.BlockSpec((d, n), lambda i: (0, 0), pipeline_mode=pl.Buffered(1)),
            table_spec, table_spec, table_spec,
        ],
        out_specs=pl.BlockSpec((tm, n), lambda i: (i, 0)),
        out_shape=jax.ShapeDtypeStruct((m, n), _BF16),
        scratch_shapes=[pltpu.VMEM((tm, d), _BF16)],
        compiler_params=pltpu.CompilerParams(
            dimension_semantics=("parallel",), vmem_limit_bytes=limit),
        name="in_proj",
    )(x, g.reshape(1, d), w, *tables)


def _rope_tables(seq_len):
    inv = 1.0 / (ROPE_THETA ** (jnp.arange(0, DIFF_DK, 2, dtype=_F32) / DIFF_DK))
    ang = jnp.arange(seq_len, dtype=_F32)[:, None] * inv[None, :]
    ang = jnp.tile(ang, (1, LANES // (DIFF_DK // 2)))
    cos, sin = jnp.cos(ang), jnp.sin(ang)
    first_half = jnp.asarray((np.arange(LANES) % DIFF_DK) < DIFF_DK // 2)[None, :]
    sin_up = jnp.where(first_half, -sin, 0.0)
    sin_dn = jnp.where(first_half, 0.0, sin)
    return cos, sin_up, sin_dn


def _na_window(variant, i):
    if variant == 0:
        return 0, NA_WIN_ROWS - 1 - i
    if variant == 1:
        return i, NA_WIN_ROWS // 2 - 1 - i
    return NA_K_ROWS - NA_WIN_ROWS, -1 - i


def _na_bias_kernel(rpb_ref, o_ref):
    h = pl.program_id(0)
    n_dc = 2 * NA_WIN_COLS - 1
    c = lax.broadcasted_iota(jnp.int32, (GRID_W, LANES), 0)
    lane = lax.broadcasted_iota(jnp.int32, (GRID_W, LANES), 1)
    kc = lane & (GRID_W - 1)
    cs = jnp.clip(c - NA_WIN_COLS // 2, 0, GRID_W - NA_WIN_COLS)
    col_ok = jnp.logical_and(kc >= cs, kc < cs + NA_WIN_COLS)
    dc = kc - c + (NA_WIN_COLS - 1)
    neg = jnp.full((GRID_W, LANES), MASK_VALUE, _F32)
    by_dr = []
    for d in range(2 * NA_WIN_ROWS - 1):
        u = neg
        for n in range(n_dc):
            u = jnp.where(dc == n, rpb_ref[h, d * n_dc + n], u)
        by_dr.append(jnp.where(col_ok, u, neg))
    low_half = lane < GRID_W
    for variant in range(3):
        for i in range(NA_Q_ROWS):
            first, dr0 = _na_window(variant, i)
            for t in range(NA_K_ROWS // 2):
                halves = []
                for jj in (2 * t, 2 * t + 1):
                    valid = first <= jj < first + NA_WIN_ROWS
                    halves.append(by_dr[dr0 + jj] if valid else neg)
                tile = (halves[0] if halves[0] is halves[1]
                        else jnp.where(low_half, halves[0], halves[1]))
                o_ref[variant, i * GRID_W:(i + 1) * GRID_W, t * LANES:(t + 1) * LANES] = tile


def _na_bias(rpb):
    n_tab = (2 * NA_WIN_ROWS - 1) * (2 * NA_WIN_COLS - 1)
    return pl.pallas_call(
        _na_bias_kernel,
        grid=(NA_HEADS,),
        in_specs=[pl.BlockSpec(memory_space=pltpu.SMEM)],
        out_specs=pl.BlockSpec((None, 3, NA_TQ, NA_TK), lambda h: (h, 0, 0, 0)),
        out_shape=jax.ShapeDtypeStruct((NA_HEADS, 3, NA_TQ, NA_TK), _F32),
        compiler_params=pltpu.CompilerParams(dimension_semantics=("parallel",)),
        name="na_bias",
    )(rpb.reshape(NA_HEADS, n_tab))


def _na_attn_kernel(q_ref, k_ref, v_ref, bias_ref, o_ref, *, rows):
    last = pl.num_programs(2) * NA_BLOCKS_PER_STEP - 1
    for u in range(NA_BLOCKS_PER_STEP):
        r = pl.program_id(2) * NA_BLOCKS_PER_STEP + u
        kr0 = jnp.clip(r * NA_Q_ROWS - NA_WIN_ROWS // 2, 0, rows - NA_K_ROWS)
        start = pl.multiple_of(kr0 * GRID_W, GRID_W)
        kb = k_ref[pl.ds(start, NA_TK), :]
        vb = v_ref[pl.ds(start, NA_TK), :]
        variant = jnp.where(r == 0, 0, jnp.where(r == last, 2, 1))
        q = q_ref[u * NA_TQ:(u + 1) * NA_TQ, :]
        s = lax.dot_general(q, kb, _NT, preferred_element_type=_F32)
        s = s * (NA_HD ** -0.5) + bias_ref[variant]
        p = jnp.exp(s - jnp.max(s, axis=-1, keepdims=True))
        l = jnp.sum(p, axis=-1, keepdims=True)
        o = jnp.dot(p.astype(_BF16), vb, preferred_element_type=_F32) / l
        o_ref[u * NA_TQ:(u + 1) * NA_TQ, :] = o.astype(o_ref.dtype)


def _na_attn(proj, bias, *, col_q, col_k, col_v):
    b, t, _ = proj.shape
    rows = t // GRID_W
    tq = NA_BLOCKS_PER_STEP * NA_TQ
    assert t % tq == 0 and rows >= NA_K_ROWS + NA_Q_ROWS
    limit = _vmem_limit(2 * _nbytes((t, NA_HD), _BF16), _nbytes((3, NA_TQ, NA_TK), _F32),
                        2 * _nbytes((tq, NA_HD), _BF16),
                        2 * NA_BLOCKS_PER_STEP * _nbytes((NA_TQ, NA_TK), _F32))
    return pl.pallas_call(
        functools.partial(_na_attn_kernel, rows=rows),
        grid=(b, NA_HEADS, t // tq),
        in_specs=[
            pl.BlockSpec((None, tq, NA_HD), lambda bi, h, r: (bi, r, col_q + h)),
            pl.BlockSpec((None, t, NA_HD), lambda bi, h, r: (bi, 0, col_k + h)),
            pl.BlockSpec((None, t, NA_HD), lambda bi, h, r: (bi, 0, col_v + h)),
            pl.BlockSpec((None, 3, NA_TQ, NA_TK), lambda bi, h, r: (h, 0, 0, 0)),
        ],
        out_specs=pl.BlockSpec((None, tq, NA_HD), lambda bi, h, r: (bi, r, h)),
        out_shape=jax.ShapeDtypeStruct((b, t, NA_HEADS * NA_HD), _BF16),
        compiler_params=pltpu.CompilerParams(
            dimension_semantics=("parallel", "parallel", "arbitrary"), vmem_limit_bytes=limit),
        name="na_attn",
    )(proj, proj, proj, bias)


def _diff_attn_kernel(q_ref, k_ref, v_ref, lq1_ref, lk1_ref, lq2_ref, lk2_ref, g_ref, o_ref,
                      vt_ref, acc1_ref, acc2_ref, *, tk, chunks_per_trip):
    n_chunks = vt_ref.shape[0]

    @pl.when(pl.program_id(2) == 0)
    def _transpose_values():
        row = lax.broadcasted_iota(jnp.int32, (BF16_SUBLANES, tk), 0)
        ones_tile = jnp.where(row == 0, 1.0, 0.0).astype(vt_ref.dtype)

        def body(c, carry):
            off = pl.multiple_of(c * tk, tk)
            vt_ref[c, :DIFF_DV, :] = v_ref[pl.ds(off, tk), :].astype(_F32).T.astype(vt_ref.dtype)
            vt_ref[c, DIFF_DV:, :] = ones_tile
            return carry
        lax.fori_loop(0, n_chunks, body, 0)

    q = q_ref[...]
    tq = q.shape[0]
    lane = lax.broadcasted_iota(jnp.int32, q.shape, 1)
    zero = jnp.zeros_like(q)
    qa = jnp.where(lane < DIFF_DK, q, zero)
    qb = jnp.where(lane >= DIFF_DK, q, zero)
    acc_refs = (acc1_ref, acc2_ref)

    def chunk_scores(c):
        off = pl.multiple_of(c * tk, tk)
        kc = k_ref[pl.ds(off, tk), :]
        return [lax.dot_general(kc, qh, _NT, preferred_element_type=_F32) for qh in (qa, qb)]

    def weighted_values(c, s, m):
        return jnp.dot(vt_ref[c], jnp.exp2(s - m).astype(_BF16), preferred_element_type=_F32)

    def first_chunk():
        m = []
        for s, acc_ref in zip(chunk_scores(0), acc_refs):
            m.append(jnp.max(s, axis=0, keepdims=True))
            acc_ref[...] = weighted_values(0, s, m[-1])
        return tuple(m)

    def lazy_chunk(c, state):
        new_state = []
        for s, acc_ref, (top, lag) in zip(chunk_scores(c), acc_refs, state):
            m = jnp.maximum(top, 0.0)
            pv = weighted_values(c, s, m)
            chunk_max = jnp.max(s, axis=0, keepdims=True)
            acc_ref[...] = (acc_ref[...] + pv) * jnp.exp2(m - jnp.maximum(m, chunk_max))
            new_state.append((jnp.maximum(top, chunk_max), jnp.maximum(lag, chunk_max - m)))
        return tuple(new_state)

    def max_first_chunk(c, m_old):
        m_out = []
        for s, acc_ref, m in zip(chunk_scores(c), acc_refs, m_old):
            m_new = jnp.maximum(m, jnp.max(s, axis=0, keepdims=True))
            acc_ref[...] = jnp.exp2(m - m_new) * acc_ref[...] + weighted_values(c, s, m_new)
            m_out.append(m_new)
        return tuple(m_out)

    n_trips, n_tail = divmod(n_chunks, chunks_per_trip)

    def fast_body(j, state):
        for u in range(chunks_per_trip):
            state = lazy_chunk(j * chunks_per_trip + u, state)
        return state

    for acc_ref in acc_refs:
        acc_ref[...] = jnp.zeros_like(acc_ref)
    start = (jnp.full((1, tq), MASK_VALUE, _F32), jnp.zeros((1, tq), _F32))
    state = lax.fori_loop(0, n_trips, fast_body, (start, start))
    for c in range(n_chunks - n_tail, n_chunks):
        state = lazy_chunk(c, state)
    worst_lag = jnp.max(jnp.maximum(jnp.maximum(state[0][1], -state[0][0]),
                                    jnp.maximum(state[1][1], -state[1][0])))

    @pl.when(jnp.logical_not(worst_lag <= DIFF_MAX_LAG))
    def _slow_path():
        lax.fori_loop(1, n_chunks, max_first_chunk, first_chunk())

    lam = (jnp.exp(jnp.sum(lq1_ref[...] * lk1_ref[...], axis=-1, keepdims=True))
           - jnp.exp(jnp.sum(lq2_ref[...] * lk2_ref[...], axis=-1, keepdims=True)) + LAM_INIT)
    o1 = acc1_ref[:DIFF_DV, :] / acc1_ref[DIFF_DV:DIFF_DV + 1, :]
    o2 = acc2_ref[:DIFF_DV, :] / acc2_ref[DIFF_DV:DIFF_DV + 1, :]
    o = o1 - lam * o2
    inv = lax.rsqrt(jnp.mean(o * o, axis=0, keepdims=True) + SUBLN_EPS)
    o = ((o * inv) * g_ref[...]) * (1.0 - LAM_INIT)
    o_ref[...] = o.T.astype(o_ref.dtype)


def _diff_attn(proj, lam_q1, lam_k1, lam_q2, lam_k2, g_subln, *, col_q, col_k, col_v, tq, tk):
    b, t, _ = proj.shape
    assert t % tq == 0 and t % tk == 0 and t // tk >= 2
    lam_spec = pl.BlockSpec((1, DIFF_DK), lambda bi, h, i: (0, 0))
    limit = _vmem_limit(2 * _nbytes((t, LANES), _BF16), 2 * _nbytes((tq, LANES), _BF16),
                        _nbytes((t, LANES), _BF16) // 2, _nbytes((DIFF_DV, tq), _F32),
                        2 * _nbytes((tk, tq), _F32))
    return pl.pallas_call(
        functools.partial(_diff_attn_kernel, tk=tk, chunks_per_trip=DIFF_CHUNKS_PER_TRIP),
        grid=(b, DIFF_HEADS, t // tq),
        in_specs=[
            pl.BlockSpec((None, tq, LANES), lambda bi, h, i: (bi, i, col_q + h)),
            pl.BlockSpec((None, t, LANES), lambda bi, h, i: (bi, 0, col_k + h)),
            pl.BlockSpec((None, t, LANES), lambda bi, h, i: (bi, 0, col_v + h)),
            lam_spec, lam_spec, lam_spec, lam_spec,
            pl.BlockSpec((DIFF_DV, 1), lambda bi, h, i: (0, 0)),
        ],
        out_specs=pl.BlockSpec((None, tq, DIFF_DV), lambda bi, h, i: (bi, i, h)),
        out_shape=jax.ShapeDtypeStruct((b, t, DIFF_HEADS * DIFF_DV), _BF16),
        scratch_shapes=[
            pltpu.VMEM((t // tk, DIFF_ACC_ROWS, tk), _BF16),
            pltpu.VMEM((DIFF_ACC_ROWS, tq), _F32),
            pltpu.VMEM((DIFF_ACC_ROWS, tq), _F32),
        ],
        compiler_params=pltpu.CompilerParams(
            dimension_semantics=("parallel", "parallel", "arbitrary"), vmem_limit_bytes=limit),
        name="diff_attn",
    )(proj, proj, proj, lam_q1.reshape(1, DIFF_DK), lam_k1.reshape(1, DIFF_DK),
      lam_q2.reshape(1, DIFF_DK), lam_k2.reshape(1, DIFF_DK), g_subln.reshape(DIFF_DV, 1))


def _out_proj_kernel(x_ref, na_ref, df_ref, w_ref, o_ref):
    n_na = na_ref.shape[1]
    acc = jnp.dot(na_ref[...], w_ref[:n_na, :], preferred_element_type=_F32)
    acc += jnp.dot(df_ref[...], w_ref[n_na:, :], preferred_element_type=_F32)
    o_ref[...] = x_ref[...] + acc


def _out_proj(x, na_o, df_o, w_out, *, tm, tn):
    m, d = x.shape
    k_na, k_df = na_o.shape[1], df_o.shape[1]
    assert m % tm == 0 and d % tn == 0 and w_out.shape == (k_na + k_df, d)
    limit = _vmem_limit(2 * _nbytes((tm, tn), _F32), _nbytes((tm, k_na + k_df), _BF16),
                        _nbytes((k_na + k_df, tn), _BF16))
    return pl.pallas_call(
        _out_proj_kernel,
        grid=(m // tm, d // tn),
        in_specs=[
            pl.BlockSpec((tm, tn), lambda i, j: (i, j)),
            pl.BlockSpec((tm, k_na), lambda i, j: (i, 0)),
            pl.BlockSpec((tm, k_df), lambda i, j: (i, 0)),
            pl.BlockSpec((k_na + k_df, tn), lambda i, j: (0, j)),
        ],
        out_specs=pl.BlockSpec((tm, tn), lambda i, j: (i, j)),
        out_shape=jax.ShapeDtypeStruct((m, d), _F32),
        compiler_params=pltpu.CompilerParams(
            dimension_semantics=("parallel", "parallel"), vmem_limit_bytes=limit),
        name="out_proj",
    )(x, na_o, df_o, w_out)


def _mem_attn_kernel(q_ref, kv_ref, h_ref, w_ref, o_ref, ctx_ref):
    d = q_ref.shape[1]
    hd = d // MEM_HEADS
    for h in range(MEM_HEADS):
        qh = q_ref[:, h * hd:(h + 1) * hd]
        kh = kv_ref[:, h * hd:(h + 1) * hd]
        vh = kv_ref[:, d + h * hd:d + (h + 1) * hd]
        s = lax.dot_general(qh, kh, _NT, preferred_element_type=_F32) * (hd ** -0.5)
        p = jnp.exp(s - jnp.max(s, axis=-1, keepdims=True))
        l = jnp.sum(p, axis=-1, keepdims=True)
        ctx = jnp.dot(p.astype(_BF16), vh, preferred_element_type=_F32) / l
        ctx_ref[:, h * hd:(h + 1) * hd] = ctx.astype(ctx_ref.dtype)
    o_ref[...] = h_ref[...] + jnp.dot(ctx_ref[...], w_ref[...], preferred_element_type=_F32)


def _mem_attn(q, kv, h1, w_mo, *, seq_len, mem_tokens, tm):
    m, d = h1.shape
    assert m % tm == 0 and seq_len % tm == 0
    tiles_per_seq = seq_len // tm
    limit = _vmem_limit(_nbytes((tm, d), _BF16), _nbytes((mem_tokens, 2 * d), _BF16),
                        2 * _nbytes((tm, d), _F32), _nbytes((d, d), _BF16),
                        _nbytes((tm, d), _BF16) // 2)
    return pl.pallas_call(
        _mem_attn_kernel,
        grid=(m // tm,),
        in_specs=[
            pl.BlockSpec((tm, d), lambda i: (i, 0)),
            pl.BlockSpec((mem_tokens, 2 * d), lambda i: (i // tiles_per_seq, 0)),
            pl.BlockSpec((tm, d), lambda i: (i, 0)),
            pl.BlockSpec((d, d), lambda i: (0, 0)),
        ],
        out_specs=pl.BlockSpec((tm, d), lambda i: (i, 0)),
        out_shape=jax.ShapeDtypeStruct((m, d), _F32),
        scratch_shapes=[pltpu.VMEM((tm, d), _BF16)],
        compiler_params=pltpu.CompilerParams(
            dimension_semantics=("parallel",), vmem_limit_bytes=limit),
        name="mem_attn",
    )(q, kv, h1, w_mo)


def _ffn_kernel(h_ref, g_ref, wg_ref, wu_ref, wd_ref, gf_ref, o_ref, hn_ref):
    f = pl.program_id(1)

    def down_projection(hn):
        gate = jnp.dot(hn, wg_ref[...], preferred_element_type=_F32)
        up = jnp.dot(hn, wu_ref[...], preferred_element_type=_F32)
        act = (gate * jax.nn.sigmoid(gate)) * up
        return jnp.dot(act.astype(_BF16), wd_ref[...], preferred_element_type=_F32)

    @pl.when(f == 0)
    def _first():
        x = h_ref[...]
        inv = lax.rsqrt(jnp.mean(x * x, axis=-1, keepdims=True) + RMS_EPS)
        hn = ((x * inv) * g_ref[...]).astype(hn_ref.dtype)
        hn_ref[...] = hn
        o_ref[...] = x + down_projection(hn)

    @pl.when(f > 0)
    def _accumulate():
        o_ref[...] += down_projection(hn_ref[...])

    @pl.when(f == pl.num_programs(1) - 1)
    def _finish():
        y = o_ref[...]
        inv = lax.rsqrt(jnp.mean(y * y, axis=-1, keepdims=True) + RMS_EPS)
        o_ref[...] = (y * inv) * gf_ref[...]


def _ffn(h2, g_ffn, w_gate_up, w_down, g_final, *, tm, tf):
    m, d = h2.shape
    d_ff = w_down.shape[0]
    assert m % tm == 0 and d_ff % tf == 0 and w_gate_up.shape == (d, 2 * d_ff)
    n_f = d_ff // tf
    limit = _vmem_limit(2 * _nbytes((tm, d), _F32), 3 * _nbytes((d, tf), _BF16),
                        _nbytes((tm, d), _BF16) // 2)
    return pl.pallas_call(
        _ffn_kernel,
        grid=(m // tm, n_f),
        in_specs=[
            pl.BlockSpec((tm, d), lambda i, f: (i, 0)),
            pl.BlockSpec((1, d), lambda i, f: (0, 0)),
            pl.BlockSpec((d, tf), lambda i, f: (0, f)),
            pl.BlockSpec((d, tf), lambda i, f: (0, n_f + f)),
            pl.BlockSpec((tf, d), lambda i, f: (f, 0)),
            pl.BlockSpec((1, d), lambda i, f: (0, 0)),
        ],
        out_specs=pl.BlockSpec((tm, d), lambda i, f: (i, 0)),
        out_shape=jax.ShapeDtypeStruct((m, d), _F32),
        scratch_shapes=[pltpu.VMEM((tm, d), _BF16)],
        compiler_params=pltpu.CompilerParams(
            dimension_semantics=("parallel", "arbitrary"), vmem_limit_bytes=limit),
        name="ffn",
    )(h2, g_ffn.reshape(1, d), w_gate_up, w_gate_up, w_down, g_final.reshape(1, d))


def _trunk(x, mem, p, na_bias):
    b, t, d = x.shape
    mem_tokens = mem.shape[1]
    m = b * t
    x2 = x.reshape(m, d)
    na_w = NA_HEADS * NA_HD
    qk_w = DIFF_HEADS * 2 * DIFF_DK
    q_cols = (3 * na_w, 3 * na_w + qk_w)
    k_cols = (3 * na_w + qk_w, 3 * na_w + 2 * qk_w)
    proj = _in_proj(x2, p["g_mix"], p["w_in"], _rope_tables(t), seq_len=t, tm=512,
                    q_cols=q_cols, k_cols=k_cols)
    proj = proj.reshape(b, t, -1)
    blk = lambda col: col // LANES
    na_o = _na_attn(proj, na_bias, col_q=0, col_k=blk(na_w), col_v=blk(2 * na_w))
    df_o = _diff_attn(proj, p["lam_q1"], p["lam_k1"], p["lam_q2"], p["lam_k2"], p["g_subln"],
                      col_q=blk(3 * na_w), col_k=blk(3 * na_w + qk_w),
                      col_v=blk(3 * na_w + 2 * qk_w), tq=1024, tk=512)
    h1 = _out_proj(x2, na_o.reshape(m, -1), df_o.reshape(m, -1), p["w_out"], tm=512, tn=d)
    q_mem = _norm_matmul(h1, p["g_xattn"], p["w_mq"], tm=512, tn=d, name="mem_q")
    kv_mem = _norm_matmul(mem.reshape(b * mem_tokens, d), p["g_mem"], p["w_mkv"],
                          tm=min(512, b * mem_tokens), tn=512, name="mem_kv")
    h2 = _mem_attn(q_mem, kv_mem, h1, p["w_mo"], seq_len=t, mem_tokens=mem_tokens, tm=512)
    y = _ffn(h2, p["g_ffn"], p["w_gate_up"], p["w_down"], p["g_final"], tm=1024, tf=512)
    return y.reshape(b, t, d)


def kernel(x_prompt, x_sample, mem_prompt, mem_sample, g_mix, w_in, rpb, lam_q1, lam_k1,
           lam_q2, lam_k2, g_subln, w_out, g_xattn, g_mem, w_mq, w_mkv, w_mo, g_ffn,
           w_gate_up, w_down, g_final):
    assert w_in.shape[0] == 1, "single-layer trunk"
    p = dict(
        g_mix=g_mix[0], w_in=w_in[0].astype(_BF16), lam_q1=lam_q1[0], lam_k1=lam_k1[0],
        lam_q2=lam_q2[0], lam_k2=lam_k2[0], g_subln=g_subln[0], w_out=w_out[0].astype(_BF16),
        g_xattn=g_xattn[0], g_mem=g_mem[0], w_mq=w_mq[0].astype(_BF16),
        w_mkv=w_mkv[0].astype(_BF16), w_mo=w_mo[0].astype(_BF16), g_ffn=g_ffn[0],
        w_gate_up=w_gate_up[0].astype(_BF16), w_down=w_down[0].astype(_BF16), g_final=g_final,
    )
    na_bias = _na_bias(rpb[0])
    return (_trunk(x_prompt, mem_prompt, p, na_bias), _trunk(x_sample, mem_sample, p, na_bias))
```

```python
import functools
import math

import jax
import jax.numpy as jnp
import numpy as np
from jax import lax
from jax.experimental import pallas as pl
from jax.experimental.pallas import tpu as pltpu

GRID_W = 64
NA_HEADS = 8
NA_HD = 128
NA_WIN_ROWS = 8
NA_WIN_COLS = 16
DIFF_HEADS = 8
DIFF_DK = 64
DIFF_DV = 2 * DIFF_DK
ROPE_THETA = 10000.0
MEM_HEADS = 4
RMS_EPS = 1e-6
SUBLN_EPS = 1e-5
LAM_INIT = 0.8 - 0.6 * math.exp(-0.3 * 0)
LOG2_E = math.log2(math.e)
DIFF_Q_SCALE = DIFF_DK ** -0.5 * LOG2_E
BF16_SUBLANES = 16
DIFF_ACC_ROWS = DIFF_DV + BF16_SUBLANES
DIFF_CHUNKS_PER_TRIP = 16
DIFF_MAX_LAG = 64.0

LANES = 128
MXU_COLS = 256
V7X_VMEM_BYTES = 64 * 1024 * 1024
VMEM_LIMIT_CAP = V7X_VMEM_BYTES - 6 * 1024 * 1024
INTERNAL_SCRATCH_ALLOWANCE = 12 * 1024 * 1024

NA_Q_ROWS = 4
NA_K_ROWS = NA_Q_ROWS + NA_WIN_ROWS
NA_TQ = NA_Q_ROWS * GRID_W
NA_TK = NA_K_ROWS * GRID_W
NA_BLOCKS_PER_STEP = 16
ROPE_ROW_GROUP = 256
MASK_VALUE = -1e30

_NT = (((1,), (1,)), ((), ()))

_BF16 = jnp.bfloat16
_F32 = jnp.float32


def _vmem_limit(*block_bytes):
    return int(min(VMEM_LIMIT_CAP, 2 * sum(block_bytes) + INTERNAL_SCRATCH_ALLOWANCE))


def _nbytes(shape, dtype):
    return int(np.prod(shape)) * jnp.dtype(dtype).itemsize


def _rms_normalize(x_ref, g_ref):
    x = x_ref[...]
    inv = lax.rsqrt(jnp.mean(x * x, axis=-1, keepdims=True) + RMS_EPS)
    return ((x * inv) * g_ref[...]).astype(_BF16)


def _norm_matmul_kernel(x_ref, g_ref, w_ref, o_ref, *scratch):
    if not scratch:
        acc = jnp.dot(_rms_normalize(x_ref, g_ref), w_ref[...], preferred_element_type=_F32)
        o_ref[...] = acc.astype(o_ref.dtype)
        return

    (xn_ref,) = scratch
    j = pl.program_id(1)

    @pl.when(j == 0)
    def _first():
        xn = _rms_normalize(x_ref, g_ref)
        xn_ref[...] = xn
        o_ref[...] = jnp.dot(xn, w_ref[...], preferred_element_type=_F32).astype(o_ref.dtype)

    @pl.when(j > 0)
    def _rest():
        acc = jnp.dot(xn_ref[...], w_ref[...], preferred_element_type=_F32)
        o_ref[...] = acc.astype(o_ref.dtype)


def _norm_matmul(x, g, w, *, tm, tn, name):
    m, d = x.shape
    n = w.shape[1]
    assert m % tm == 0 and n % tn == 0
    limit = _vmem_limit(_nbytes((tm, d), _F32), _nbytes((d, tn), _BF16), _nbytes((tm, tn), _BF16),
                        _nbytes((tm, d), _BF16) // 2, _nbytes((tm, tn), _F32))
    return pl.pallas_call(
        _norm_matmul_kernel,
        grid=(m // tm, n // tn),
        in_specs=[
            pl.BlockSpec((tm, d), lambda i, j: (i, 0)),
            pl.BlockSpec((1, d), lambda i, j: (0, 0)),
            pl.BlockSpec((d, tn), lambda i, j: (0, j)),
        ],
        out_specs=pl.BlockSpec((tm, tn), lambda i, j: (i, j)),
        out_shape=jax.ShapeDtypeStruct((m, n), _BF16),
        scratch_shapes=[] if n == tn else [pltpu.VMEM((tm, d), _BF16)],
        compiler_params=pltpu.CompilerParams(
            dimension_semantics=("parallel", "arbitrary"), vmem_limit_bytes=limit),
        name=name,
    )(x, g.reshape(1, d), w)


def _in_proj_kernel(x_ref, g_ref, w_ref, cos_ref, sin_up_ref, sin_dn_ref, o_ref, xn_ref, *,
                    q_cols, k_cols):
    xn_ref[...] = _rms_normalize(x_ref, g_ref)
    tm, n = o_ref.shape
    rows = min(tm, ROPE_ROW_GROUP)
    for n0 in range(0, n, MXU_COLS):
        cols = slice(n0, n0 + MXU_COLS)
        is_q = q_cols[0] <= n0 < q_cols[1]
        if not (is_q or k_cols[0] <= n0 < k_cols[1]):
            acc = jnp.dot(xn_ref[...], w_ref[:, cols], preferred_element_type=_F32)
            o_ref[:, cols] = acc.astype(o_ref.dtype)
            continue
        for r0 in range(0, tm, rows):
            rs = slice(r0, r0 + rows)
            cos, sin_up, sin_dn = cos_ref[rs, :], sin_up_ref[rs, :], sin_dn_ref[rs, :]
            acc = jnp.dot(xn_ref[rs, :], w_ref[:, cols], preferred_element_type=_F32)
            for c0 in range(0, MXU_COLS, LANES):
                a = acc[:, c0:c0 + LANES]
                r = (a * cos + pltpu.roll(a, LANES - DIFF_DK // 2, 1) * sin_up
                     + pltpu.roll(a, DIFF_DK // 2, 1) * sin_dn)
                if is_q:
                    r = r * DIFF_Q_SCALE
                o_ref[rs, n0 + c0:n0 + c0 + LANES] = r.astype(o_ref.dtype)


def _in_proj(x, g, w, tables, *, seq_len, tm, q_cols, k_cols):
    m, d = x.shape
    n = w.shape[1]
    assert m % tm == 0 and seq_len % tm == 0 and n % MXU_COLS == 0
    assert all(c % MXU_COLS == 0 for c in q_cols + k_cols)
    tiles_per_seq = seq_len // tm
    table_spec = pl.BlockSpec((tm, LANES), lambda i: (i % tiles_per_seq, 0))
    limit = _vmem_limit(_nbytes((tm, d), _F32), _nbytes((d, n), _BF16) // 2,
                        _nbytes((tm, n), _BF16), 3 * _nbytes((tm, LANES), _F32),
                        _nbytes((tm, d), _BF16) // 2)
    return pl.pallas_call(
        functools.partial(_in_proj_kernel, q_cols=q_cols, k_cols=k_cols),
        grid=(m // tm,),
        in_specs=[
            pl.BlockSpec((tm, d), lambda i: (i, 0)),
            pl.BlockSpec((1, d), lambda i: (0, 0)),
            pl.BlockSpec((d, n), lambda i: (0, 0), pipeline_mode=pl.Buffered(1)),
            table_spec, table_spec, table_spec,
        ],
        out_specs=pl.BlockSpec((tm, n), lambda i: (i, 0)),
        out_shape=jax.ShapeDtypeStruct((m, n), _BF16),
        scratch_shapes=[pltpu.VMEM((tm, d), _BF16)],
        compiler_params=pltpu.CompilerParams(
            dimension_semantics=("parallel",), vmem_limit_bytes=limit),
        name="in_proj",
    )(x, g.reshape(1, d), w, *tables)


def _rope_tables(seq_len):
    inv = 1.0 / (ROPE_THETA ** (jnp.arange(0, DIFF_DK, 2, dtype=_F32) / DIFF_DK))
    ang = jnp.arange(seq_len, dtype=_F32)[:, None] * inv[None, :]
    ang = jnp.tile(ang, (1, LANES // (DIFF_DK // 2)))
    cos, sin = jnp.cos(ang), jnp.sin(ang)
    first_half = jnp.asarray((np.arange(LANES) % DIFF_DK) < DIFF_DK // 2)[None, :]
    sin_up = jnp.where(first_half, -sin, 0.0)
    sin_dn = jnp.where(first_half, 0.0, sin)
    return cos, sin_up, sin_dn


def _na_window(variant, i):
    if variant == 0:
        return 0, NA_WIN_ROWS - 1 - i
    if variant == 1:
        return i, NA_WIN_ROWS // 2 - 1 - i
    return NA_K_ROWS - NA_WIN_ROWS, -1 - i


def _na_bias_kernel(rpb_ref, o_ref):
    h = pl.program_id(0)
    n_dc = 2 * NA_WIN_COLS - 1
    c = lax.broadcasted_iota(jnp.int32, (GRID_W, LANES), 0)
    lane = lax.broadcasted_iota(jnp.int32, (GRID_W, LANES), 1)
    kc = lane & (GRID_W - 1)
    cs = jnp.clip(c - NA_WIN_COLS // 2, 0, GRID_W - NA_WIN_COLS)
    col_ok = jnp.logical_and(kc >= cs, kc < cs + NA_WIN_COLS)
    dc = kc - c + (NA_WIN_COLS - 1)
    neg = jnp.full((GRID_W, LANES), MASK_VALUE, _F32)
    by_dr = []
    for d in range(2 * NA_WIN_ROWS - 1):
        u = neg
        for n in range(n_dc):
            u = jnp.where(dc == n, rpb_ref[h, d * n_dc + n], u)
        by_dr.append(jnp.where(col_ok, u, neg))
    low_half = lane < GRID_W
    for variant in range(3):
        for i in range(NA_Q_ROWS):
            first, dr0 = _na_window(variant, i)
            for t in range(NA_K_ROWS // 2):
                halves = []
                for jj in (2 * t, 2 * t + 1):
                    valid = first <= jj < first + NA_WIN_ROWS
                    halves.append(by_dr[dr0 + jj] if valid else neg)
                tile = (halves[0] if halves[0] is halves[1]
                        else jnp.where(low_half, halves[0], halves[1]))
                o_ref[variant, i * GRID_W:(i + 1) * GRID_W, t * LANES:(t + 1) * LANES] = tile


def _na_bias(rpb):
    n_tab = (2 * NA_WIN_ROWS - 1) * (2 * NA_WIN_COLS - 1)
    return pl.pallas_call(
        _na_bias_kernel,
        grid=(NA_HEADS,),
        in_specs=[pl.BlockSpec(memory_space=pltpu.SMEM)],
        out_specs=pl.BlockSpec((None, 3, NA_TQ, NA_TK), lambda h: (h, 0, 0, 0)),
        out_shape=jax.ShapeDtypeStruct((NA_HEADS, 3, NA_TQ, NA_TK), _F32),
        compiler_params=pltpu.CompilerParams(dimension_semantics=("parallel",)),
        name="na_bias",
    )(rpb.reshape(NA_HEADS, n_tab))


def _na_attn_kernel(q_ref, k_ref, v_ref, bias_ref, o_ref, *, rows):
    last = pl.num_programs(2) * NA_BLOCKS_PER_STEP - 1
    for u in range(NA_BLOCKS_PER_STEP):
        r = pl.program_id(2) * NA_BLOCKS_PER_STEP + u
        kr0 = jnp.clip(r * NA_Q_ROWS - NA_WIN_ROWS // 2, 0, rows - NA_K_ROWS)
        start = pl.multiple_of(kr0 * GRID_W, GRID_W)
        kb = k_ref[pl.ds(start, NA_TK), :]
        vb = v_ref[pl.ds(start, NA_TK), :]
        variant = jnp.where(r == 0, 0, jnp.where(r == last, 2, 1))
        q = q_ref[u * NA_TQ:(u + 1) * NA_TQ, :]
        s = lax.dot_general(q, kb, _NT, preferred_element_type=_F32)
        s = s * (NA_HD ** -0.5) + bias_ref[variant]
        p = jnp.exp(s - jnp.max(s, axis=-1, keepdims=True))
        l = jnp.sum(p, axis=-1, keepdims=True)
        o = jnp.dot(p.astype(_BF16), vb, preferred_element_type=_F32) / l
        o_ref[u * NA_TQ:(u + 1) * NA_TQ, :] = o.astype(o_ref.dtype)


def _na_attn(proj, bias, *, col_q, col_k, col_v):
    b, t, _ = proj.shape
    rows = t // GRID_W
    tq = NA_BLOCKS_PER_STEP * NA_TQ
    assert t % tq == 0 and rows >= NA_K_ROWS + NA_Q_ROWS
    limit = _vmem_limit(2 * _nbytes((t, NA_HD), _BF16), _nbytes((3, NA_TQ, NA_TK), _F32),
                        2 * _nbytes((tq, NA_HD), _BF16),
                        2 * NA_BLOCKS_PER_STEP * _nbytes((NA_TQ, NA_TK), _F32))
    return pl.pallas_call(
        functools.partial(_na_attn_kernel, rows=rows),
        grid=(b, NA_HEADS, t // tq),
        in_specs=[
            pl.BlockSpec((None, tq, NA_HD), lambda bi, h, r: (bi, r, col_q + h)),
            pl.BlockSpec((None, t, NA_HD), lambda bi, h, r: (bi, 0, col_k + h)),
            pl.BlockSpec((None, t, NA_HD), lambda bi, h, r: (bi, 0, col_v + h)),
            pl.BlockSpec((None, 3, NA_TQ, NA_TK), lambda bi, h, r: (h, 0, 0, 0)),
        ],
        out_specs=pl.BlockSpec((None, tq, NA_HD), lambda bi, h, r: (bi, r, h)),
        out_shape=jax.ShapeDtypeStruct((b, t, NA_HEADS * NA_HD), _BF16),
        compiler_params=pltpu.CompilerParams(
            dimension_semantics=("parallel", "parallel", "arbitrary"), vmem_limit_bytes=limit),
        name="na_attn",
    )(proj, proj, proj, bias)


def _diff_attn_kernel(q_ref, k_ref, v_ref, lq1_ref, lk1_ref, lq2_ref, lk2_ref, g_ref, o_ref,
                      vt_ref, acc1_ref, acc2_ref, *, tk, chunks_per_trip):
    n_chunks = vt_ref.shape[0]

    @pl.when(pl.program_id(2) == 0)
    def _transpose_values():
        row = lax.broadcasted_iota(jnp.int32, (BF16_SUBLANES, tk), 0)
        ones_tile = jnp.where(row == 0, 1.0, 0.0).astype(vt_ref.dtype)

        def body(c, carry):
            off = pl.multiple_of(c * tk, tk)
            vt_ref[c, :DIFF_DV, :] = v_ref[pl.ds(off, tk), :].astype(_F32).T.astype(vt_ref.dtype)
            vt_ref[c, DIFF_DV:, :] = ones_tile
            return carry
        lax.fori_loop(0, n_chunks, body, 0)

    q = q_ref[...]
    tq = q.shape[0]
    lane = lax.broadcasted_iota(jnp.int32, q.shape, 1)
    zero = jnp.zeros_like(q)
    qa = jnp.where(lane < DIFF_DK, q, zero)
    qb = jnp.where(lane >= DIFF_DK, q, zero)
    acc_refs = (acc1_ref, acc2_ref)

    def chunk_scores(c):
        off = pl.multiple_of(c * tk, tk)
        kc = k_ref[pl.ds(off, tk), :]
        return [lax.dot_general(kc, qh, _NT, preferred_element_type=_F32) for qh in (qa, qb)]

    def weighted_values(c, s, m):
        return jnp.dot(vt_ref[c], jnp.exp2(s - m).astype(_BF16), preferred_element_type=_F32)

    def first_chunk():
        m = []
        for s, acc_ref in zip(chunk_scores(0), acc_refs):
            m.append(jnp.max(s, axis=0, keepdims=True))
            acc_ref[...] = weighted_values(0, s, m[-1])
        return tuple(m)

    def lazy_chunk(c, state):
        new_state = []
        for s, acc_ref, (top, lag) in zip(chunk_scores(c), acc_refs, state):
            m = jnp.maximum(top, 0.0)
            pv = weighted_values(c, s, m)
            chunk_max = jnp.max(s, axis=0, keepdims=True)
            acc_ref[...] = (acc_ref[...] + pv) * jnp.exp2(m - jnp.maximum(m, chunk_max))
            new_state.append((jnp.maximum(top, chunk_max), jnp.maximum(lag, chunk_max - m)))
        return tuple(new_state)

    def max_first_chunk(c, m_old):
        m_out = []
        for s, acc_ref, m in zip(chunk_scores(c), acc_refs, m_old):
            m_new = jnp.maximum(m, jnp.max(s, axis=0, keepdims=True))
            acc_ref[...] = jnp.exp2(m - m_new) * acc_ref[...] + weighted_values(c, s, m_new)
            m_out.append(m_new)
        return tuple(m_out)

    n_trips, n_tail = divmod(n_chunks, chunks_per_trip)

    def fast_body(j, state):
        for u in range(chunks_per_trip):
            state = lazy_chunk(j * chunks_per_trip + u, state)
        return state

    for acc_ref in acc_refs:
        acc_ref[...] = jnp.zeros_like(acc_ref)
    start = (jnp.full((1, tq), MASK_VALUE, _F32), jnp.zeros((1, tq), _F32))
    state = lax.fori_loop(0, n_trips, fast_body, (start, start))
    for c in range(n_chunks - n_tail, n_chunks):
        state = lazy_chunk(c, state)
    worst_lag = jnp.max(jnp.maximum(jnp.maximum(state[0][1], -state[0][0]),
                                    jnp.maximum(state[1][1], -state[1][0])))

    @pl.when(jnp.logical_not(worst_lag <= DIFF_MAX_LAG))
    def _slow_path():
        lax.fori_loop(1, n_chunks, max_first_chunk, first_chunk())

    lam = (jnp.exp(jnp.sum(lq1_ref[...] * lk1_ref[...], axis=-1, keepdims=True))
           - jnp.exp(jnp.sum(lq2_ref[...] * lk2_ref[...], axis=-1, keepdims=True)) + LAM_INIT)
    o1 = acc1_ref[:DIFF_DV, :] / acc1_ref[DIFF_DV:DIFF_DV + 1, :]
    o2 = acc2_ref[:DIFF_DV, :] / acc2_ref[DIFF_DV:DIFF_DV + 1, :]
    o = o1 - lam * o2
    inv = lax.rsqrt(jnp.mean(o * o, axis=0, keepdims=True) + SUBLN_EPS)
    o = ((o * inv) * g_ref[...]) * (1.0 - LAM_INIT)
    o_ref[...] = o.T.astype(o_ref.dtype)


def _diff_attn(proj, lam_q1, lam_k1, lam_q2, lam_k2, g_subln, *, col_q, col_k, col_v, tq, tk):
    b, t, _ = proj.shape
    assert t % tq == 0 and t % tk == 0 and t // tk >= 2
    lam_spec = pl.BlockSpec((1, DIFF_DK), lambda bi, h, i: (0, 0))
    limit = _vmem_limit(2 * _nbytes((t, LANES), _BF16), 2 * _nbytes((tq, LANES), _BF16),
                        _nbytes((t, LANES), _BF16) // 2, _nbytes((DIFF_DV, tq), _F32),
                        2 * _nbytes((tk, tq), _F32))
    return pl.pallas_call(
        functools.partial(_diff_attn_kernel, tk=tk, chunks_per_trip=DIFF_CHUNKS_PER_TRIP),
        grid=(b, DIFF_HEADS, t // tq),
        in_specs=[
            pl.BlockSpec((None, tq, LANES), lambda bi, h, i: (bi, i, col_q + h)),
            pl.BlockSpec((None, t, LANES), lambda bi, h, i: (bi, 0, col_k + h)),
            pl.BlockSpec((None, t, LANES), lambda bi, h, i: (bi, 0, col_v + h)),
            lam_spec, lam_spec, lam_spec, lam_spec,
            pl.BlockSpec((DIFF_DV, 1), lambda bi, h, i: (0, 0)),
        ],
        out_specs=pl.BlockSpec((None, tq, DIFF_DV), lambda bi, h, i: (bi, i, h)),
        out_shape=jax.ShapeDtypeStruct((b, t, DIFF_HEADS * DIFF_DV), _BF16),
        scratch_shapes=[
            pltpu.VMEM((t // tk, DIFF_ACC_ROWS, tk), _BF16),
            pltpu.VMEM((DIFF_ACC_ROWS, tq), _F32),
            pltpu.VMEM((DIFF_ACC_ROWS, tq), _F32),
        ],
        compiler_params=pltpu.CompilerParams(
            dimension_semantics=("parallel", "parallel", "arbitrary"), vmem_limit_bytes=limit),
        name="diff_attn",
    )(proj, proj, proj, lam_q1.reshape(1, DIFF_DK), lam_k1.reshape(1, DIFF_DK),
      lam_q2.reshape(1, DIFF_DK), lam_k2.reshape(1, DIFF_DK), g_subln.reshape(DIFF_DV, 1))


def _out_proj_kernel(x_ref, na_ref, df_ref, w_ref, o_ref):
    n_na = na_ref.shape[1]
    acc = jnp.dot(na_ref[...], w_ref[:n_na, :], preferred_element_type=_F32)
    acc += jnp.dot(df_ref[...], w_ref[n_na:, :], preferred_element_type=_F32)
    o_ref[...] = x_ref[...] + acc


def _out_proj(x, na_o, df_o, w_out, *, tm, tn):
    m, d = x.shape
    k_na, k_df = na_o.shape[1], df_o.shape[1]
    assert m % tm == 0 and d % tn == 0 and w_out.shape == (k_na + k_df, d)
    limit = _vmem_limit(2 * _nbytes((tm, tn), _F32), _nbytes((tm, k_na + k_df), _BF16),
                        _nbytes((k_na + k_df, tn), _BF16))
    return pl.pallas_call(
        _out_proj_kernel,
        grid=(m // tm, d // tn),
        in_specs=[
            pl.BlockSpec((tm, tn), lambda i, j: (i, j)),
            pl.BlockSpec((tm, k_na), lambda i, j: (i, 0)),
            pl.BlockSpec((tm, k_df), lambda i, j: (i, 0)),
            pl.BlockSpec((k_na + k_df, tn), lambda i, j: (0, j)),
        ],
        out_specs=pl.BlockSpec((tm, tn), lambda i, j: (i, j)),
        out_shape=jax.ShapeDtypeStruct((m, d), _F32),
        compiler_params=pltpu.CompilerParams(
            dimension_semantics=("parallel", "parallel"), vmem_limit_bytes=limit),
        name="out_proj",
    )(x, na_o, df_o, w_out)


def _mem_attn_kernel(q_ref, kv_ref, h_ref, w_ref, o_ref, ctx_ref):
    d = q_ref.shape[1]
    hd = d // MEM_HEADS
    for h in range(MEM_HEADS):
        qh = q_ref[:, h * hd:(h + 1) * hd]
        kh = kv_ref[:, h * hd:(h + 1) * hd]
        vh = kv_ref[:, d + h * hd:d + (h + 1) * hd]
        s = lax.dot_general(qh, kh, _NT, preferred_element_type=_F32) * (hd ** -0.5)
        p = jnp.exp(s - jnp.max(s, axis=-1, keepdims=True))
        l = jnp.sum(p, axis=-1, keepdims=True)
        ctx = jnp.dot(p.astype(_BF16), vh, preferred_element_type=_F32) / l
        ctx_ref[:, h * hd:(h + 1) * hd] = ctx.astype(ctx_ref.dtype)
    o_ref[...] = h_ref[...] + jnp.dot(ctx_ref[...], w_ref[...], preferred_element_type=_F32)


def _mem_attn(q, kv, h1, w_mo, *, seq_len, mem_tokens, tm):
    m, d = h1.shape
    assert m % tm == 0 and seq_len % tm == 0
    tiles_per_seq = seq_len // tm
    limit = _vmem_limit(_nbytes((tm, d), _BF16), _nbytes((mem_tokens, 2 * d), _BF16),
                        2 * _nbytes((tm, d), _F32), _nbytes((d, d), _BF16),
                        _nbytes((tm, d), _BF16) // 2)
    return pl.pallas_call(
        _mem_attn_kernel,
        grid=(m // tm,),
        in_specs=[
            pl.BlockSpec((tm, d), lambda i: (i, 0)),
            pl.BlockSpec((mem_tokens, 2 * d), lambda i: (i // tiles_per_seq, 0)),
            pl.BlockSpec((tm, d), lambda i: (i, 0)),
            pl.BlockSpec((d, d), lambda i: (0, 0)),
        ],
        out_specs=pl.BlockSpec((tm, d), lambda i: (i, 0)),
        out_shape=jax.ShapeDtypeStruct((m, d), _F32),
        scratch_shapes=[pltpu.VMEM((tm, d), _BF16)],
        compiler_params=pltpu.CompilerParams(
            dimension_semantics=("parallel",), vmem_limit_bytes=limit),
        name="mem_attn",
    )(q, kv, h1, w_mo)


def _ffn_kernel(h_ref, g_ref, wg_ref, wu_ref, wd_ref, gf_ref, o_ref, hn_ref):
    f = pl.program_id(1)

    def down_projection(hn):
        gate = jnp.dot(hn, wg_ref[...], preferred_element_type=_F32)
        up = jnp.dot(hn, wu_ref[...], preferred_element_type=_F32)
        act = (gate * jax.nn.sigmoid(gate)) * up
        return jnp.dot(act.astype(_BF16), wd_ref[...], preferred_element_type=_F32)

    @pl.when(f == 0)
    def _first():
        x = h_ref[...]
        inv = lax.rsqrt(jnp.mean(x * x, axis=-1, keepdims=True) + RMS_EPS)
        hn = ((x * inv) * g_ref[...]).astype(hn_ref.dtype)
        hn_ref[...] = hn
        o_ref[...] = x + down_projection(hn)

    @pl.when(f > 0)
    def _accumulate():
        o_ref[...] += down_projection(hn_ref[...])

    @pl.when(f == pl.num_programs(1) - 1)
    def _finish():
        y = o_ref[...]
        inv = lax.rsqrt(jnp.mean(y * y, axis=-1, keepdims=True) + RMS_EPS)
        o_ref[...] = (y * inv) * gf_ref[...]


def _ffn(h2, g_ffn, w_gate_up, w_down, g_final, *, tm, tf):
    m, d = h2.shape
    d_ff = w_down.shape[0]
    assert m % tm == 0 and d_ff % tf == 0 and w_gate_up.shape == (d, 2 * d_ff)
    n_f = d_ff // tf
    limit = _vmem_limit(2 * _nbytes((tm, d), _F32), 3 * _nbytes((d, tf), _BF16),
                        _nbytes((tm, d), _BF16) // 2)
    return pl.pallas_call(
        _ffn_kernel,
        grid=(m // tm, n_f),
        in_specs=[
            pl.BlockSpec((tm, d), lambda i, f: (i, 0)),
            pl.BlockSpec((1, d), lambda i, f: (0, 0)),
            pl.BlockSpec((d, tf), lambda i, f: (0, f)),
            pl.BlockSpec((d, tf), lambda i, f: (0, n_f + f)),
            pl.BlockSpec((tf, d), lambda i, f: (f, 0)),
            pl.BlockSpec((1, d), lambda i, f: (0, 0)),
        ],
        out_specs=pl.BlockSpec((tm, d), lambda i, f: (i, 0)),
        out_shape=jax.ShapeDtypeStruct((m, d), _F32),
        scratch_shapes=[pltpu.VMEM((tm, d), _BF16)],
        compiler_params=pltpu.CompilerParams(
            dimension_semantics=("parallel", "arbitrary"), vmem_limit_bytes=limit),
        name="ffn",
    )(h2, g_ffn.reshape(1, d), w_gate_up, w_gate_up, w_down, g_final.reshape(1, d))


def _trunk(x, mem, p, na_bias):
    b, t, d = x.shape
    mem_tokens = mem.shape[1]
    m = b * t
    x2 = x.reshape(m, d)
    na_w = NA_HEADS * NA_HD
    qk_w = DIFF_HEADS * 2 * DIFF_DK
    q_cols = (3 * na_w, 3 * na_w + qk_w)
    k_cols = (3 * na_w + qk_w, 3 * na_w + 2 * qk_w)
    proj = _in_proj(x2, p["g_mix"], p["w_in"], _rope_tables(t), seq_len=t, tm=512,
                    q_cols=q_cols, k_cols=k_cols)
    proj = proj.reshape(b, t, -1)
    blk = lambda col: col // LANES
    na_o = _na_attn(proj, na_bias, col_q=0, col_k=blk(na_w), col_v=blk(2 * na_w))
    df_o = _diff_attn(proj, p["lam_q1"], p["lam_k1"], p["lam_q2"], p["lam_k2"], p["g_subln"],
                      col_q=blk(3 * na_w), col_k=blk(3 * na_w + qk_w),
                      col_v=blk(3 * na_w + 2 * qk_w), tq=1024, tk=512)
    h1 = _out_proj(x2, na_o.reshape(m, -1), df_o.reshape(m, -1), p["w_out"], tm=512, tn=d)
    q_mem = _norm_matmul(h1, p["g_xattn"], p["w_mq"], tm=512, tn=d, name="mem_q")
    kv_mem = _norm_matmul(mem.reshape(b * mem_tokens, d), p["g_mem"], p["w_mkv"],
                          tm=min(512, b * mem_tokens), tn=512, name="mem_kv")
    h2 = _mem_attn(q_mem, kv_mem, h1, p["w_mo"], seq_len=t, mem_tokens=mem_tokens, tm=512)
    y = _ffn(h2, p["g_ffn"], p["w_gate_up"], p["w_down"], p["g_final"], tm=1024, tf=512)
    return y.reshape(b, t, d)


def kernel(x_prompt, x_sample, mem_prompt, mem_sample, g_mix, w_in, rpb, lam_q1, lam_k1,
           lam_q2, lam_k2, g_subln, w_out, g_xattn, g_mem, w_mq, w_mkv, w_mo, g_ffn,
           w_gate_up, w_down, g_final):
    assert w_in.shape[0] == 1, "single-layer trunk"
    p = dict(
        g_mix=g_mix[0], w_in=w_in[0].astype(_BF16), lam_q1=lam_q1[0], lam_k1=lam_k1[0],
        lam_q2=lam_q2[0], lam_k2=lam_k2[0], g_subln=g_subln[0], w_out=w_out[0].astype(_BF16),
        g_xattn=g_xattn[0], g_mem=g_mem[0], w_mq=w_mq[0].astype(_BF16),
        w_mkv=w_mkv[0].astype(_BF16), w_mo=w_mo[0].astype(_BF16), g_ffn=g_ffn[0],
        w_gate_up=w_gate_up[0].astype(_BF16), w_down=w_down[0].astype(_BF16), g_final=g_final,
    )
    na_bias = _na_bias(rpb[0])
    return (_trunk(x_prompt, mem_prompt, p, na_bias), _trunk(x_sample, mem_sample, p, na_bias))
```

```python
import functools
import math

import jax
import jax.numpy as jnp
import numpy as np
from jax import lax
from jax.experimental import pallas as pl
from jax.experimental.pallas import tpu as pltpu

GRID_W = 64
NA_HEADS = 8
NA_HD = 128
NA_WIN_ROWS = 8
NA_WIN_COLS = 16
DIFF_HEADS = 8
DIFF_DK = 64
DIFF_DV = 2 * DIFF_DK
ROPE_THETA = 10000.0
MEM_HEADS = 4
RMS_EPS = 1e-6
SUBLN_EPS = 1e-5
LAM_INIT = 0.8 - 0.6 * math.exp(-0.3 * 0)
LOG2_E = math.log2(math.e)
DIFF_Q_SCALE = DIFF_DK ** -0.5 * LOG2_E
BF16_SUBLANES = 16
DIFF_ACC_ROWS = DIFF_DV + BF16_SUBLANES
DIFF_CHUNKS_PER_TRIP = 8
DIFF_MAX_LAG = 64.0

LANES = 128
MXU_COLS = 256
V7X_VMEM_BYTES = 64 * 1024 * 1024
VMEM_LIMIT_CAP = V7X_VMEM_BYTES - 6 * 1024 * 1024
INTERNAL_SCRATCH_ALLOWANCE = 12 * 1024 * 1024

NA_Q_ROWS = 4
NA_K_ROWS = NA_Q_ROWS + NA_WIN_ROWS
NA_TQ = NA_Q_ROWS * GRID_W
NA_TK = NA_K_ROWS * GRID_W
NA_BLOCKS_PER_STEP = 16
ROPE_ROW_GROUP = 256
MASK_VALUE = -1e30

_NT = (((1,), (1,)), ((), ()))

_BF16 = jnp.bfloat16
_F32 = jnp.float32


def _vmem_limit(*block_bytes):
    return int(min(VMEM_LIMIT_CAP, 2 * sum(block_bytes) + INTERNAL_SCRATCH_ALLOWANCE))


def _nbytes(shape, dtype):
    return int(np.prod(shape)) * jnp.dtype(dtype).itemsize


def _rms_normalize(x_ref, g_ref):
    x = x_ref[...]
    inv = lax.rsqrt(jnp.mean(x * x, axis=-1, keepdims=True) + RMS_EPS)
    return ((x * inv) * g_ref[...]).astype(_BF16)


def _norm_matmul_kernel(x_ref, g_ref, w_ref, o_ref, *scratch):
    if not scratch:
        acc = jnp.dot(_rms_normalize(x_ref, g_ref), w_ref[...], preferred_element_type=_F32)
        o_ref[...] = acc.astype(o_ref.dtype)
        return

    (xn_ref,) = scratch
    j = pl.program_id(1)

    @pl.when(j == 0)
    def _first():
        xn = _rms_normalize(x_ref, g_ref)
        xn_ref[...] = xn
        o_ref[...] = jnp.dot(xn, w_ref[...], preferred_element_type=_F32).astype(o_ref.dtype)

    @pl.when(j > 0)
    def _rest():
        acc = jnp.dot(xn_ref[...], w_ref[...], preferred_element_type=_F32)
        o_ref[...] = acc.astype(o_ref.dtype)


def _norm_matmul(x, g, w, *, tm, tn, name):
    m, d = x.shape
    n = w.shape[1]
    assert m % tm == 0 and n % tn == 0
    limit = _vmem_limit(_nbytes((tm, d), _F32), _nbytes((d, tn), _BF16), _nbytes((tm, tn), _BF16),
                        _nbytes((tm, d), _BF16) // 2, _nbytes((tm, tn), _F32))
    return pl.pallas_call(
        _norm_matmul_kernel,
        grid=(m // tm, n // tn),
        in_specs=[
            pl.BlockSpec((tm, d), lambda i, j: (i, 0)),
            pl.BlockSpec((1, d), lambda i, j: (0, 0)),
            pl.BlockSpec((d, tn), lambda i, j: (0, j)),
        ],
        out_specs=pl.BlockSpec((tm, tn), lambda i, j: (i, j)),
        out_shape=jax.ShapeDtypeStruct((m, n), _BF16),
        scratch_shapes=[] if n == tn else [pltpu.VMEM((tm, d), _BF16)],
        compiler_params=pltpu.CompilerParams(
            dimension_semantics=("parallel", "arbitrary"), vmem_limit_bytes=limit),
        name=name,
    )(x, g.reshape(1, d), w)


def _in_proj_kernel(x_ref, g_ref, w_ref, cos_ref, sin_up_ref, sin_dn_ref, o_ref, xn_ref, *,
                    q_cols, k_cols):
    xn_ref[...] = _rms_normalize(x_ref, g_ref)
    tm, n = o_ref.shape
    rows = min(tm, ROPE_ROW_GROUP)
    for n0 in range(0, n, MXU_COLS):
        cols = slice(n0, n0 + MXU_COLS)
        is_q = q_cols[0] <= n0 < q_cols[1]
        if not (is_q or k_cols[0] <= n0 < k_cols[1]):
            acc = jnp.dot(xn_ref[...], w_ref[:, cols], preferred_element_type=_F32)
            o_ref[:, cols] = acc.astype(o_ref.dtype)
            continue
        for r0 in range(0, tm, rows):
            rs = slice(r0, r0 + rows)
            cos, sin_up, sin_dn = cos_ref[rs, :], sin_up_ref[rs, :], sin_dn_ref[rs, :]
            acc = jnp.dot(xn_ref[rs, :], w_ref[:, cols], preferred_element_type=_F32)
            for c0 in range(0, MXU_COLS, LANES):
                a = acc[:, c0:c0 + LANES]
                r = (a * cos + pltpu.roll(a, LANES - DIFF_DK // 2, 1) * sin_up
                     + pltpu.roll(a, DIFF_DK // 2, 1) * sin_dn)
                if is_q:
                    r = r * DIFF_Q_SCALE
                o_ref[rs, n0 + c0:n0 + c0 + LANES] = r.astype(o_ref.dtype)


def _in_proj(x, g, w, tables, *, seq_len, tm, q_cols, k_cols):
    m, d = x.shape
    n = w.shape[1]
    assert m % tm == 0 and seq_len % tm == 0 and n % MXU_COLS == 0
    assert all(c % MXU_COLS == 0 for c in q_cols + k_cols)
    tiles_per_seq = seq_len // tm
    table_spec = pl.BlockSpec((tm, LANES), lambda i: (i % tiles_per_seq, 0))
    limit = _vmem_limit(_nbytes((tm, d), _F32), _nbytes((d, n), _BF16) // 2,
                        _nbytes((tm, n), _BF16), 3 * _nbytes((tm, LANES), _F32),
                        _nbytes((tm, d), _BF16) // 2)
    return pl.pallas_call(
        functools.partial(_in_proj_kernel, q_cols=q_cols, k_cols=k_cols),
        grid=(m // tm,),
        in_specs=[
            pl.BlockSpec((tm, d), lambda i: (i, 0)),
            pl.BlockSpec((1, d), lambda i: (0, 0)),
            pl.BlockSpec((d, n), lambda i: (0, 0), pipeline_mode=pl.Buffered(1)),
            table_spec, table_spec, table_spec,
        ],
        out_specs=pl.BlockSpec((tm, n), lambda i: (i, 0)),
        out_shape=jax.ShapeDtypeStruct((m, n), _BF16),
        scratch_shapes=[pltpu.VMEM((tm, d), _BF16)],
        compiler_params=pltpu.CompilerParams(
            dimension_semantics=("parallel",), vmem_limit_bytes=limit),
        name="in_proj",
    )(x, g.reshape(1, d), w, *tables)


def _rope_tables(seq_len):
    inv = 1.0 / (ROPE_THETA ** (jnp.arange(0, DIFF_DK, 2, dtype=_F32) / DIFF_DK))
    ang = jnp.arange(seq_len, dtype=_F32)[:, None] * inv[None, :]
    ang = jnp.tile(ang, (1, LANES // (DIFF_DK // 2)))
    cos, sin = jnp.cos(ang), jnp.sin(ang)
    first_half = jnp.asarray((np.arange(LANES) % DIFF_DK) < DIFF_DK // 2)[None, :]
    sin_up = jnp.where(first_half, -sin, 0.0)
    sin_dn = jnp.where(first_half, 0.0, sin)
    return cos, sin_up, sin_dn


def _na_window(variant, i):
    if variant == 0:
        return 0, NA_WIN_ROWS - 1 - i
    if variant == 1:
        return i, NA_WIN_ROWS // 2 - 1 - i
    return NA_K_ROWS - NA_WIN_ROWS, -1 - i


def _na_bias_kernel(rpb_ref, o_ref):
    h = pl.program_id(0)
    n_dc = 2 * NA_WIN_COLS - 1
    c = lax.broadcasted_iota(jnp.int32, (GRID_W, LANES), 0)
    lane = lax.broadcasted_iota(jnp.int32, (GRID_W, LANES), 1)
    kc = lane & (GRID_W - 1)
    cs = jnp.clip(c - NA_WIN_COLS // 2, 0, GRID_W - NA_WIN_COLS)
    col_ok = jnp.logical_and(kc >= cs, kc < cs + NA_WIN_COLS)
    dc = kc - c + (NA_WIN_COLS - 1)
    neg = jnp.full((GRID_W, LANES), MASK_VALUE, _F32)
    by_dr = []
    for d in range(2 * NA_WIN_ROWS - 1):
        u = neg
        for n in range(n_dc):
            u = jnp.where(dc == n, rpb_ref[h, d * n_dc + n], u)
        by_dr.append(jnp.where(col_ok, u, neg))
    low_half = lane < GRID_W
    for variant in range(3):
        for i in range(NA_Q_ROWS):
            first, dr0 = _na_window(variant, i)
            for t in range(NA_K_ROWS // 2):
                halves = []
                for jj in (2 * t, 2 * t + 1):
                    valid = first <= jj < first + NA_WIN_ROWS
                    halves.append(by_dr[dr0 + jj] if valid else neg)
                tile = (halves[0] if halves[0] is halves[1]
                        else jnp.where(low_half, halves[0], halves[1]))
                o_ref[variant, i * GRID_W:(i + 1) * GRID_W, t * LANES:(t + 1) * LANES] = tile


def _na_bias(rpb):
    n_tab = (2 * NA_WIN_ROWS - 1) * (2 * NA_WIN_COLS - 1)
    return pl.pallas_call(
        _na_bias_kernel,
        grid=(NA_HEADS,),
        in_specs=[pl.BlockSpec(memory_space=pltpu.SMEM)],
        out_specs=pl.BlockSpec((None, 3, NA_TQ, NA_TK), lambda h: (h, 0, 0, 0)),
        out_shape=jax.ShapeDtypeStruct((NA_HEADS, 3, NA_TQ, NA_TK), _F32),
        compiler_params=pltpu.CompilerParams(dimension_semantics=("parallel",)),
        name="na_bias",
    )(rpb.reshape(NA_HEADS, n_tab))


def _na_attn_kernel(q_ref, k_ref, v_ref, bias_ref, o_ref, *, rows):
    last = pl.num_programs(2) * NA_BLOCKS_PER_STEP - 1
    for u in range(NA_BLOCKS_PER_STEP):
        r = pl.program_id(2) * NA_BLOCKS_PER_STEP + u
        kr0 = jnp.clip(r * NA_Q_ROWS - NA_WIN_ROWS // 2, 0, rows - NA_K_ROWS)
        start = pl.multiple_of(kr0 * GRID_W, GRID_W)
        kb = k_ref[pl.ds(start, NA_TK), :]
        vb = v_ref[pl.ds(start, NA_TK), :]
        variant = jnp.where(r == 0, 0, jnp.where(r == last, 2, 1))
        q = q_ref[u * NA_TQ:(u + 1) * NA_TQ, :]
        s = lax.dot_general(q, kb, _NT, preferred_element_type=_F32)
        s = s * (NA_HD ** -0.5) + bias_ref[variant]
        p = jnp.exp(s - jnp.max(s, axis=-1, keepdims=True))
        l = jnp.sum(p, axis=-1, keepdims=True)
        o = jnp.dot(p.astype(_BF16), vb, preferred_element_type=_F32) / l
        o_ref[u * NA_TQ:(u + 1) * NA_TQ, :] = o.astype(o_ref.dtype)


def _na_attn(proj, bias, *, col_q, col_k, col_v):
    b, t, _ = proj.shape
    rows = t // GRID_W
    tq = NA_BLOCKS_PER_STEP * NA_TQ
    assert t % tq == 0 and rows >= NA_K_ROWS + NA_Q_ROWS
    limit = _vmem_limit(2 * _nbytes((t, NA_HD), _BF16), _nbytes((3, NA_TQ, NA_TK), _F32),
                        2 * _nbytes((tq, NA_HD), _BF16),
                        2 * NA_BLOCKS_PER_STEP * _nbytes((NA_TQ, NA_TK), _F32))
    return pl.pallas_call(
        functools.partial(_na_attn_kernel, rows=rows),
        grid=(b, NA_HEADS, t // tq),
        in_specs=[
            pl.BlockSpec((None, tq, NA_HD), lambda bi, h, r: (bi, r, col_q + h)),
            pl.BlockSpec((None, t, NA_HD), lambda bi, h, r: (bi, 0, col_k + h)),
            pl.BlockSpec((None, t, NA_HD), lambda bi, h, r: (bi, 0, col_v + h)),
            pl.BlockSpec((None, 3, NA_TQ, NA_TK), lambda bi, h, r: (h, 0, 0, 0)),
        ],
        out_specs=pl.BlockSpec((None, tq, NA_HD), lambda bi, h, r: (bi, r, h)),
        out_shape=jax.ShapeDtypeStruct((b, t, NA_HEADS * NA_HD), _BF16),
        compiler_params=pltpu.CompilerParams(
            dimension_semantics=("parallel", "parallel", "arbitrary"), vmem_limit_bytes=limit),
        name="na_attn",
    )(proj, proj, proj, bias)


def _diff_attn_kernel(q_ref, k_ref, v_ref, lq1_ref, lk1_ref, lq2_ref, lk2_ref, g_ref, o_ref,
                      vt_ref, acc1_ref, acc2_ref, *, tk, chunks_per_trip):
    n_chunks = vt_ref.shape[0]

    @pl.when(pl.program_id(2) == 0)
    def _transpose_values():
        row = lax.broadcasted_iota(jnp.int32, (BF16_SUBLANES, tk), 0)
        ones_tile = jnp.where(row == 0, 1.0, 0.0).astype(vt_ref.dtype)

        def body(c, carry):
            off = pl.multiple_of(c * tk, tk)
            vt_ref[c, :DIFF_DV, :] = v_ref[pl.ds(off, tk), :].astype(_F32).T.astype(vt_ref.dtype)
            vt_ref[c, DIFF_DV:, :] = ones_tile
            return carry
        lax.fori_loop(0, n_chunks, body, 0)

    q = q_ref[...]
    tq = q.shape[0]
    lane = lax.broadcasted_iota(jnp.int32, q.shape, 1)
    zero = jnp.zeros_like(q)
    qa = jnp.where(lane < DIFF_DK, q, zero)
    qb = jnp.where(lane >= DIFF_DK, q, zero)
    acc_refs = (acc1_ref, acc2_ref)

    def chunk_scores(c):
        off = pl.multiple_of(c * tk, tk)
        kc = k_ref[pl.ds(off, tk), :]
        return [lax.dot_general(kc, qh, _NT, preferred_element_type=_F32) for qh in (qa, qb)]

    def weighted_values(c, s, m):
        return jnp.dot(vt_ref[c], jnp.exp2(s - m).astype(_BF16), preferred_element_type=_F32)

    def first_chunk():
        m = []
        for s, acc_ref in zip(chunk_scores(0), acc_refs):
            m.append(jnp.max(s, axis=0, keepdims=True))
            acc_ref[...] = weighted_values(0, s, m[-1])
        return tuple(m)

    def lazy_chunk(c, state):
        new_state = []
        for s, acc_ref, (top, lag) in zip(chunk_scores(c), acc_refs, state):
            m = jnp.maximum(top, 0.0)
            pv = weighted_values(c, s, m)
            chunk_max = jnp.max(s, axis=0, keepdims=True)
            acc_ref[...] = (acc_ref[...] + pv) * jnp.exp2(m - jnp.maximum(m, chunk_max))
            new_state.append((jnp.maximum(top, chunk_max), jnp.maximum(lag, chunk_max - m)))
        return tuple(new_state)

    def max_first_chunk(c, m_old):
        m_out = []
        for s, acc_ref, m in zip(chunk_scores(c), acc_refs, m_old):
            m_new = jnp.maximum(m, jnp.max(s, axis=0, keepdims=True))
            acc_ref[...] = jnp.exp2(m - m_new) * acc_ref[...] + weighted_values(c, s, m_new)
            m_out.append(m_new)
        return tuple(m_out)

    n_trips, n_tail = divmod(n_chunks, chunks_per_trip)

    def fast_body(j, state):
        for u in range(chunks_per_trip):
            state = lazy_chunk(j * chunks_per_trip + u, state)
        return state

    for acc_ref in acc_refs:
        acc_ref[...] = jnp.zeros_like(acc_ref)
    start = (jnp.full((1, tq), MASK_VALUE, _F32), jnp.zeros((1, tq), _F32))
    state = lax.fori_loop(0, n_trips, fast_body, (start, start))
    for c in range(n_chunks - n_tail, n_chunks):
        state = lazy_chunk(c, state)
    worst_lag = jnp.max(jnp.maximum(jnp.maximum(state[0][1], -state[0][0]),
                                    jnp.maximum(state[1][1], -state[1][0])))

    @pl.when(jnp.logical_not(worst_lag <= DIFF_MAX_LAG))
    def _slow_path():
        lax.fori_loop(1, n_chunks, max_first_chunk, first_chunk())

    lam = (jnp.exp(jnp.sum(lq1_ref[...] * lk1_ref[...], axis=-1, keepdims=True))
           - jnp.exp(jnp.sum(lq2_ref[...] * lk2_ref[...], axis=-1, keepdims=True)) + LAM_INIT)
    o1 = acc1_ref[:DIFF_DV, :] / acc1_ref[DIFF_DV:DIFF_DV + 1, :]
    o2 = acc2_ref[:DIFF_DV, :] / acc2_ref[DIFF_DV:DIFF_DV + 1, :]
    o = o1 - lam * o2
    inv = lax.rsqrt(jnp.mean(o * o, axis=0, keepdims=True) + SUBLN_EPS)
    o = ((o * inv) * g_ref[...]) * (1.0 - LAM_INIT)
    o_ref[...] = o.T.astype(o_ref.dtype)


def _diff_attn(proj, lam_q1, lam_k1, lam_q2, lam_k2, g_subln, *, col_q, col_k, col_v, tq, tk):
    b, t, _ = proj.shape
    assert t % tq == 0 and t % tk == 0 and t // tk >= 2
    lam_spec = pl.BlockSpec((1, DIFF_DK), lambda bi, h, i: (0, 0))
    limit = _vmem_limit(2 * _nbytes((t, LANES), _BF16), 2 * _nbytes((tq, LANES), _BF16),
                        _nbytes((t, LANES), _BF16) // 2, _nbytes((DIFF_DV, tq), _F32),
                        2 * _nbytes((tk, tq), _F32))
    return pl.pallas_call(
        functools.partial(_diff_attn_kernel, tk=tk, chunks_per_trip=DIFF_CHUNKS_PER_TRIP),
        grid=(b, DIFF_HEADS, t // tq),
        in_specs=[
            pl.BlockSpec((None, tq, LANES), lambda bi, h, i: (bi, i, col_q + h)),
            pl.BlockSpec((None, t, LANES), lambda bi, h, i: (bi, 0, col_k + h)),
            pl.BlockSpec((None, t, LANES), lambda bi, h, i: (bi, 0, col_v + h)),
            lam_spec, lam_spec, lam_spec, lam_spec,
            pl.BlockSpec((DIFF_DV, 1), lambda bi, h, i: (0, 0)),
        ],
        out_specs=pl.BlockSpec((None, tq, DIFF_DV), lambda bi, h, i: (bi, i, h)),
        out_shape=jax.ShapeDtypeStruct((b, t, DIFF_HEADS * DIFF_DV), _BF16),
        scratch_shapes=[
            pltpu.VMEM((t // tk, DIFF_ACC_ROWS, tk), _BF16),
            pltpu.VMEM((DIFF_ACC_ROWS, tq), _F32),
            pltpu.VMEM((DIFF_ACC_ROWS, tq), _F32),
        ],
        compiler_params=pltpu.CompilerParams(
            dimension_semantics=("parallel", "parallel", "arbitrary"), vmem_limit_bytes=limit),
        name="diff_attn",
    )(proj, proj, proj, lam_q1.reshape(1, DIFF_DK), lam_k1.reshape(1, DIFF_DK),
      lam_q2.reshape(1, DIFF_DK), lam_k2.reshape(1, DIFF_DK), g_subln.reshape(DIFF_DV, 1))


def _out_proj_kernel(x_ref, na_ref, df_ref, w_ref, o_ref):
    n_na = na_ref.shape[1]
    acc = jnp.dot(na_ref[...], w_ref[:n_na, :], preferred_element_type=_F32)
    acc += jnp.dot(df_ref[...], w_ref[n_na:, :], preferred_element_type=_F32)
    o_ref[...] = x_ref[...] + acc


def _out_proj(x, na_o, df_o, w_out, *, tm, tn):
    m, d = x.shape
    k_na, k_df = na_o.shape[1], df_o.shape[1]
    assert m % tm == 0 and d % tn == 0 and w_out.shape == (k_na + k_df, d)
    limit = _vmem_limit(2 * _nbytes((tm, tn), _F32), _nbytes((tm, k_na + k_df), _BF16),
                        _nbytes((k_na + k_df, tn), _BF16))
    return pl.pallas_call(
        _out_proj_kernel,
        grid=(m // tm, d // tn),
        in_specs=[
            pl.BlockSpec((tm, tn), lambda i, j: (i, j)),
            pl.BlockSpec((tm, k_na), lambda i, j: (i, 0)),
            pl.BlockSpec((tm, k_df), lambda i, j: (i, 0)),
            pl.BlockSpec((k_na + k_df, tn), lambda i, j: (0, j)),
        ],
        out_specs=pl.BlockSpec((tm, tn), lambda i, j: (i, j)),
        out_shape=jax.ShapeDtypeStruct((m, d), _F32),
        compiler_params=pltpu.CompilerParams(
            dimension_semantics=("parallel", "parallel"), vmem_limit_bytes=limit),
        name="out_proj",
    )(x, na_o, df_o, w_out)


def _mem_attn_kernel(q_ref, kv_ref, h_ref, w_ref, o_ref, ctx_ref):
    d = q_ref.shape[1]
    hd = d // MEM_HEADS
    for h in range(MEM_HEADS):
        qh = q_ref[:, h * hd:(h + 1) * hd]
        kh = kv_ref[:, h * hd:(h + 1) * hd]
        vh = kv_ref[:, d + h * hd:d + (h + 1) * hd]
        s = lax.dot_general(qh, kh, _NT, preferred_element_type=_F32) * (hd ** -0.5)
        p = jnp.exp(s - jnp.max(s, axis=-1, keepdims=True))
        l = jnp.sum(p, axis=-1, keepdims=True)
        ctx = jnp.dot(p.astype(_BF16), vh, preferred_element_type=_F32) / l
        ctx_ref[:, h * hd:(h + 1) * hd] = ctx.astype(ctx_ref.dtype)
    o_ref[...] = h_ref[...] + jnp.dot(ctx_ref[...], w_ref[...], preferred_element_type=_F32)


def _mem_attn(q, kv, h1, w_mo, *, seq_len, mem_tokens, tm):
    m, d = h1.shape
    assert m % tm == 0 and seq_len % tm == 0
    tiles_per_seq = seq_len // tm
    limit = _vmem_limit(_nbytes((tm, d), _BF16), _nbytes((mem_tokens, 2 * d), _BF16),
                        2 * _nbytes((tm, d), _F32), _nbytes((d, d), _BF16),
                        _nbytes((tm, d), _BF16) // 2)
    return pl.pallas_call(
        _mem_attn_kernel,
        grid=(m // tm,),
        in_specs=[
            pl.BlockSpec((tm, d), lambda i: (i, 0)),
            pl.BlockSpec((mem_tokens, 2 * d), lambda i: (i // tiles_per_seq, 0)),
            pl.BlockSpec((tm, d), lambda i: (i, 0)),
            pl.BlockSpec((d, d), lambda i: (0, 0)),
        ],
        out_specs=pl.BlockSpec((tm, d), lambda i: (i, 0)),
        out_shape=jax.ShapeDtypeStruct((m, d), _F32),
        scratch_shapes=[pltpu.VMEM((tm, d), _BF16)],
        compiler_params=pltpu.CompilerParams(
            dimension_semantics=("parallel",), vmem_limit_bytes=limit),
        name="mem_attn",
    )(q, kv, h1, w_mo)


def _ffn_kernel(h_ref, g_ref, wg_ref, wu_ref, wd_ref, gf_ref, o_ref, hn_ref):
    f = pl.program_id(1)

    def down_projection(hn):
        gate = jnp.dot(hn, wg_ref[...], preferred_element_type=_F32)
        up = jnp.dot(hn, wu_ref[...], preferred_element_type=_F32)
        act = (gate * jax.nn.sigmoid(gate)) * up
        return jnp.dot(act.astype(_BF16), wd_ref[...], preferred_element_type=_F32)

    @pl.when(f == 0)
    def _first():
        x = h_ref[...]
        inv = lax.rsqrt(jnp.mean(x * x, axis=-1, keepdims=True) + RMS_EPS)
        hn = ((x * inv) * g_ref[...]).astype(hn_ref.dtype)
        hn_ref[...] = hn
        o_ref[...] = x + down_projection(hn)

    @pl.when(f > 0)
    def _accumulate():
        o_ref[...] += down_projection(hn_ref[...])

    @pl.when(f == pl.num_programs(1) - 1)
    def _finish():
        y = o_ref[...]
        inv = lax.rsqrt(jnp.mean(y * y, axis=-1, keepdims=True) + RMS_EPS)
        o_ref[...] = (y * inv) * gf_ref[...]


def _ffn(h2, g_ffn, w_gate_up, w_down, g_final, *, tm, tf):
    m, d = h2.shape
    d_ff = w_down.shape[0]
    assert m % tm == 0 and d_ff % tf == 0 and w_gate_up.shape == (d, 2 * d_ff)
    n_f = d_ff // tf
    limit = _vmem_limit(2 * _nbytes((tm, d), _F32), 3 * _nbytes((d, tf), _BF16),
                        _nbytes((tm, d), _BF16) // 2)
    return pl.pallas_call(
        _ffn_kernel,
        grid=(m // tm, n_f),
        in_specs=[
            pl.BlockSpec((tm, d), lambda i, f: (i, 0)),
            pl.BlockSpec((1, d), lambda i, f: (0, 0)),
            pl.BlockSpec((d, tf), lambda i, f: (0, f)),
            pl.BlockSpec((d, tf), lambda i, f: (0, n_f + f)),
            pl.BlockSpec((tf, d), lambda i, f: (f, 0)),
            pl.BlockSpec((1, d), lambda i, f: (0, 0)),
        ],
        out_specs=pl.BlockSpec((tm, d), lambda i, f: (i, 0)),
        out_shape=jax.ShapeDtypeStruct((m, d), _F32),
        scratch_shapes=[pltpu.VMEM((tm, d), _BF16)],
        compiler_params=pltpu.CompilerParams(
            dimension_semantics=("parallel", "arbitrary"), vmem_limit_bytes=limit),
        name="ffn",
    )(h2, g_ffn.reshape(1, d), w_gate_up, w_gate_up, w_down, g_final.reshape(1, d))


def _trunk(x, mem, p, na_bias):
    b, t, d = x.shape
    mem_tokens = mem.shape[1]
    m = b * t
    x2 = x.reshape(m, d)
    na_w = NA_HEADS * NA_HD
    qk_w = DIFF_HEADS * 2 * DIFF_DK
    q_cols = (3 * na_w, 3 * na_w + qk_w)
    k_cols = (3 * na_w + qk_w, 3 * na_w + 2 * qk_w)
    proj = _in_proj(x2, p["g_mix"], p["w_in"], _rope_tables(t), seq_len=t, tm=512,
                    q_cols=q_cols, k_cols=k_cols)
    proj = proj.reshape(b, t, -1)
    blk = lambda col: col // LANES
    na_o = _na_attn(proj, na_bias, col_q=0, col_k=blk(na_w), col_v=blk(2 * na_w))
    df_o = _diff_attn(proj, p["lam_q1"], p["lam_k1"], p["lam_q2"], p["lam_k2"], p["g_subln"],
                      col_q=blk(3 * na_w), col_k=blk(3 * na_w + qk_w),
                      col_v=blk(3 * na_w + 2 * qk_w), tq=2048, tk=512)
    h1 = _out_proj(x2, na_o.reshape(m, -1), df_o.reshape(m, -1), p["w_out"], tm=512, tn=d)
    q_mem = _norm_matmul(h1, p["g_xattn"], p["w_mq"], tm=512, tn=d, name="mem_q")
    kv_mem = _norm_matmul(mem.reshape(b * mem_tokens, d), p["g_mem"], p["w_mkv"],
                          tm=min(512, b * mem_tokens), tn=512, name="mem_kv")
    h2 = _mem_attn(q_mem, kv_mem, h1, p["w_mo"], seq_len=t, mem_tokens=mem_tokens, tm=512)
    y = _ffn(h2, p["g_ffn"], p["w_gate_up"], p["w_down"], p["g_final"], tm=1024, tf=512)
    return y.reshape(b, t, d)


def kernel(x_prompt, x_sample, mem_prompt, mem_sample, g_mix, w_in, rpb, lam_q1, lam_k1,
           lam_q2, lam_k2, g_subln, w_out, g_xattn, g_mem, w_mq, w_mkv, w_mo, g_ffn,
           w_gate_up, w_down, g_final):
    assert w_in.shape[0] == 1, "single-layer trunk"
    p = dict(
        g_mix=g_mix[0], w_in=w_in[0].astype(_BF16), lam_q1=lam_q1[0], lam_k1=lam_k1[0],
        lam_q2=lam_q2[0], lam_k2=lam_k2[0], g_subln=g_subln[0], w_out=w_out[0].astype(_BF16),
        g_xattn=g_xattn[0], g_mem=g_mem[0], w_mq=w_mq[0].astype(_BF16),
        w_mkv=w_mkv[0].astype(_BF16), w_mo=w_mo[0].astype(_BF16), g_ffn=g_ffn[0],
        w_gate_up=w_gate_up[0].astype(_BF16), w_down=w_down[0].astype(_BF16), g_final=g_final,
    )
    na_bias = _na_bias(rpb[0])
    return (_trunk(x_prompt, mem_prompt, p, na_bias), _trunk(x_sample, mem_sample, p, na_bias))
```

```python
import functools
import math

import jax
import jax.numpy as jnp
import numpy as np
from jax import lax
from jax.experimental import pallas as pl
from jax.experimental.pallas import tpu as pltpu

GRID_W = 64
NA_HEADS = 8
NA_HD = 128
NA_WIN_ROWS = 8
NA_WIN_COLS = 16
DIFF_HEADS = 8
DIFF_DK = 64
DIFF_DV = 2 * DIFF_DK
ROPE_THETA = 10000.0
MEM_HEADS = 4
RMS_EPS = 1e-6
SUBLN_EPS = 1e-5
LAM_INIT = 0.8 - 0.6 * math.exp(-0.3 * 0)
LOG2_E = math.log2(math.e)
DIFF_Q_SCALE = DIFF_DK ** -0.5 * LOG2_E
BF16_SUBLANES = 16
DIFF_ACC_ROWS = DIFF_DV + BF16_SUBLANES
DIFF_CHUNKS_PER_TRIP = 16
DIFF_MAX_LAG = 64.0

LANES = 128
MXU_COLS = 256
V7X_VMEM_BYTES = 64 * 1024 * 1024
VMEM_LIMIT_CAP = V7X_VMEM_BYTES - 6 * 1024 * 1024
INTERNAL_SCRATCH_ALLOWANCE = 12 * 1024 * 1024

NA_Q_ROWS = 4
NA_K_ROWS = NA_Q_ROWS + NA_WIN_ROWS
NA_TQ = NA_Q_ROWS * GRID_W
NA_TK = NA_K_ROWS * GRID_W
NA_BLOCKS_PER_STEP = 16
ROPE_ROW_GROUP = 256
MASK_VALUE = -1e30

_NT = (((1,), (1,)), ((), ()))

_BF16 = jnp.bfloat16
_F32 = jnp.float32


def _vmem_limit(*block_bytes):
    return int(min(VMEM_LIMIT_CAP, 2 * sum(block_bytes) + INTERNAL_SCRATCH_ALLOWANCE))


def _nbytes(shape, dtype):
    return int(np.prod(shape)) * jnp.dtype(dtype).itemsize


def _rms_normalize(x_ref, g_ref):
    x = x_ref[...]
    inv = lax.rsqrt(jnp.mean(x * x, axis=-1, keepdims=True) + RMS_EPS)
    return ((x * inv) * g_ref[...]).astype(_BF16)


def _norm_matmul_kernel(x_ref, g_ref, w_ref, o_ref, *scratch):
    if not scratch:
        acc = jnp.dot(_rms_normalize(x_ref, g_ref), w_ref[...], preferred_element_type=_F32)
        o_ref[...] = acc.astype(o_ref.dtype)
        return

    (xn_ref,) = scratch
    j = pl.program_id(1)

    @pl.when(j == 0)
    def _first():
        xn = _rms_normalize(x_ref, g_ref)
        xn_ref[...] = xn
        o_ref[...] = jnp.dot(xn, w_ref[...], preferred_element_type=_F32).astype(o_ref.dtype)

    @pl.when(j > 0)
    def _rest():
        acc = jnp.dot(xn_ref[...], w_ref[...], preferred_element_type=_F32)
        o_ref[...] = acc.astype(o_ref.dtype)


def _norm_matmul(x, g, w, *, tm, tn, name):
    m, d = x.shape
    n = w.shape[1]
    assert m % tm == 0 and n % tn == 0
    limit = _vmem_limit(_nbytes((tm, d), _F32), _nbytes((d, tn), _BF16), _nbytes((tm, tn), _BF16),
                        _nbytes((tm, d), _BF16) // 2, _nbytes((tm, tn), _F32))
    return pl.pallas_call(
        _norm_matmul_kernel,
        grid=(m // tm, n // tn),
        in_specs=[
            pl.BlockSpec((tm, d), lambda i, j: (i, 0)),
            pl.BlockSpec((1, d), lambda i, j: (0, 0)),
            pl.BlockSpec((d, tn), lambda i, j: (0, j)),
        ],
        out_specs=pl.BlockSpec((tm, tn), lambda i, j: (i, j)),
        out_shape=jax.ShapeDtypeStruct((m, n), _BF16),
        scratch_shapes=[] if n == tn else [pltpu.VMEM((tm, d), _BF16)],
        compiler_params=pltpu.CompilerParams(
            dimension_semantics=("parallel", "arbitrary"), vmem_limit_bytes=limit),
        name=name,
    )(x, g.reshape(1, d), w)


def _in_proj_kernel(x_ref, g_ref, w_ref, cos_ref, sin_up_ref, sin_dn_ref, o_ref, xn_ref, *,
                    q_cols, k_cols):
    xn_ref[...] = _rms_normalize(x_ref, g_ref)
    tm, n = o_ref.shape
    rows = min(tm, ROPE_ROW_GROUP)
    for n0 in range(0, n, MXU_COLS):
        cols = slice(n0, n0 + MXU_COLS)
        is_q = q_cols[0] <= n0 < q_cols[1]
        if not (is_q or k_cols[0] <= n0 < k_cols[1]):
            acc = jnp.dot(xn_ref[...], w_ref[:, cols], preferred_element_type=_F32)
            o_ref[:, cols] = acc.astype(o_ref.dtype)
            continue
        for r0 in range(0, tm, rows):
            rs = slice(r0, r0 + rows)
            cos, sin_up, sin_dn = cos_ref[rs, :], sin_up_ref[rs, :], sin_dn_ref[rs, :]
            acc = jnp.dot(xn_ref[rs, :], w_ref[:, cols], preferred_element_type=_F32)
            for c0 in range(0, MXU_COLS, LANES):
                a = acc[:, c0:c0 + LANES]
                r = (a * cos + pltpu.roll(a, LANES - DIFF_DK // 2, 1) * sin_up
                     + pltpu.roll(a, DIFF_DK // 2, 1) * sin_dn)
                if is_q:
                    r = r * DIFF_Q_SCALE
                o_ref[rs, n0 + c0:n0 + c0 + LANES] = r.astype(o_ref.dtype)


def _in_proj(x, g, w, tables, *, seq_len, tm, q_cols, k_cols):
    m, d = x.shape
    n = w.shape[1]
    assert m % tm == 0 and seq_len % tm == 0 and n % MXU_COLS == 0
    assert all(c % MXU_COLS == 0 for c in q_cols + k_cols)
    tiles_per_seq = seq_len // tm
    table_spec = pl.BlockSpec((tm, LANES), lambda i: (i % tiles_per_seq, 0))
    limit = _vmem_limit(_nbytes((tm, d), _F32), _nbytes((d, n), _BF16) // 2,
                        _nbytes((tm, n), _BF16), 3 * _nbytes((tm, LANES), _F32),
                        _nbytes((tm, d), _BF16) // 2)
    return pl.pallas_call(
        functools.partial(_in_proj_kernel, q_cols=q_cols, k_cols=k_cols),
        grid=(m // tm,),
        in_specs=[
            pl.BlockSpec((tm, d), lambda i: (i, 0)),
            pl.BlockSpec((1, d), lambda i: (0, 0)),
            pl.BlockSpec((d, n), lambda i: (0, 0), pipeline_mode=pl.Buffered(1)),
            table_spec, table_spec, table_spec,
        ],
        out_specs=pl.BlockSpec((tm, n), lambda i: (i, 0)),
        out_shape=jax.ShapeDtypeStruct((m, n), _BF16),
        scratch_shapes=[pltpu.VMEM((tm, d), _BF16)],
        compiler_params=pltpu.CompilerParams(
            dimension_semantics=("parallel",), vmem_limit_bytes=limit),
        name="in_proj",
    )(x, g.reshape(1, d), w, *tables)


def _rope_tables(seq_len):
    inv = 1.0 / (ROPE_THETA ** (jnp.arange(0, DIFF_DK, 2, dtype=_F32) / DIFF_DK))
    ang = jnp.arange(seq_len, dtype=_F32)[:, None] * inv[None, :]
    ang = jnp.tile(ang, (1, LANES // (DIFF_DK // 2)))
    cos, sin = jnp.cos(ang), jnp.sin(ang)
    first_half = jnp.asarray((np.arange(LANES) % DIFF_DK) < DIFF_DK // 2)[None, :]
    sin_up = jnp.where(first_half, -sin, 0.0)
    sin_dn = jnp.where(first_half, 0.0, sin)
    return cos, sin_up, sin_dn


def _na_window(variant, i):
    if variant == 0:
        return 0, NA_WIN_ROWS - 1 - i
    if variant == 1:
        return i, NA_WIN_ROWS // 2 - 1 - i
    return NA_K_ROWS - NA_WIN_ROWS, -1 - i


def _na_bias_kernel(rpb_ref, o_ref):
    h = pl.program_id(0)
    n_dc = 2 * NA_WIN_COLS - 1
    c = lax.broadcasted_iota(jnp.int32, (GRID_W, LANES), 0)
    lane = lax.broadcasted_iota(jnp.int32, (GRID_W, LANES), 1)
    kc = lane & (GRID_W - 1)
    cs = jnp.clip(c - NA_WIN_COLS // 2, 0, GRID_W - NA_WIN_COLS)
    col_ok = jnp.logical_and(kc >= cs, kc < cs + NA_WIN_COLS)
    dc = kc - c + (NA_WIN_COLS - 1)
    neg = jnp.full((GRID_W, LANES), MASK_VALUE, _F32)
    by_dr = []
    for d in range(2 * NA_WIN_ROWS - 1):
        u = neg
        for n in range(n_dc):
            u = jnp.where(dc == n, rpb_ref[h, d * n_dc + n], u)
        by_dr.append(jnp.where(col_ok, u, neg))
    low_half = lane < GRID_W
    for variant in range(3):
        for i in range(NA_Q_ROWS):
            first, dr0 = _na_window(variant, i)
            for t in range(NA_K_ROWS // 2):
                halves = []
                for jj in (2 * t, 2 * t + 1):
                    valid = first <= jj < first + NA_WIN_ROWS
                    halves.append(by_dr[dr0 + jj] if valid else neg)
                tile = (halves[0] if halves[0] is halves[1]
                        else jnp.where(low_half, halves[0], halves[1]))
                o_ref[variant, i * GRID_W:(i + 1) * GRID_W, t * LANES:(t + 1) * LANES] = tile


def _na_bias(rpb):
    n_tab = (2 * NA_WIN_ROWS - 1) * (2 * NA_WIN_COLS - 1)
    return pl.pallas_call(
        _na_bias_kernel,
        grid=(NA_HEADS,),
        in_specs=[pl.BlockSpec(memory_space=pltpu.SMEM)],
        out_specs=pl.BlockSpec((None, 3, NA_TQ, NA_TK), lambda h: (h, 0, 0, 0)),
        out_shape=jax.ShapeDtypeStruct((NA_HEADS, 3, NA_TQ, NA_TK), _F32),
        compiler_params=pltpu.CompilerParams(dimension_semantics=("parallel",)),
        name="na_bias",
    )(rpb.reshape(NA_HEADS, n_tab))


def _na_attn_kernel(q_ref, k_ref, v_ref, bias_ref, o_ref, *, rows):
    last = pl.num_programs(2) * NA_BLOCKS_PER_STEP - 1
    for u in range(NA_BLOCKS_PER_STEP):
        r = pl.program_id(2) * NA_BLOCKS_PER_STEP + u
        kr0 = jnp.clip(r * NA_Q_ROWS - NA_WIN_ROWS // 2, 0, rows - NA_K_ROWS)
        start = pl.multiple_of(kr0 * GRID_W, GRID_W)
        kb = k_ref[pl.ds(start, NA_TK), :]
        vb = v_ref[pl.ds(start, NA_TK), :]
        variant = jnp.where(r == 0, 0, jnp.where(r == last, 2, 1))
        q = q_ref[u * NA_TQ:(u + 1) * NA_TQ, :]
        s = lax.dot_general(q, kb, _NT, preferred_element_type=_F32)
        s = s * (NA_HD ** -0.5) + bias_ref[variant]
        p = jnp.exp(s - jnp.max(s, axis=-1, keepdims=True))
        l = jnp.sum(p, axis=-1, keepdims=True)
        o = jnp.dot(p.astype(_BF16), vb, preferred_element_type=_F32) / l
        o_ref[u * NA_TQ:(u + 1) * NA_TQ, :] = o.astype(o_ref.dtype)


def _na_attn(proj, bias, *, col_q, col_k, col_v):
    b, t, _ = proj.shape
    rows = t // GRID_W
    tq = NA_BLOCKS_PER_STEP * NA_TQ
    assert t % tq == 0 and rows >= NA_K_ROWS + NA_Q_ROWS
    limit = _vmem_limit(2 * _nbytes((t, NA_HD), _BF16), _nbytes((3, NA_TQ, NA_TK), _F32),
                        2 * _nbytes((tq, NA_HD), _BF16),
                        2 * NA_BLOCKS_PER_STEP * _nbytes((NA_TQ, NA_TK), _F32))
    return pl.pallas_call(
        functools.partial(_na_attn_kernel, rows=rows),
        grid=(b, NA_HEADS, t // tq),
        in_specs=[
            pl.BlockSpec((None, tq, NA_HD), lambda bi, h, r: (bi, r, col_q + h)),
            pl.BlockSpec((None, t, NA_HD), lambda bi, h, r: (bi, 0, col_k + h)),
            pl.BlockSpec((None, t, NA_HD), lambda bi, h, r: (bi, 0, col_v + h)),
            pl.BlockSpec((None, 3, NA_TQ, NA_TK), lambda bi, h, r: (h, 0, 0, 0)),
        ],
        out_specs=pl.BlockSpec((None, tq, NA_HD), lambda bi, h, r: (bi, r, h)),
        out_shape=jax.ShapeDtypeStruct((b, t, NA_HEADS * NA_HD), _BF16),
        compiler_params=pltpu.CompilerParams(
            dimension_semantics=("parallel", "parallel", "arbitrary"), vmem_limit_bytes=limit),
        name="na_attn",
    )(proj, proj, proj, bias)


def _diff_attn_kernel(q_ref, k_ref, v_ref, lq1_ref, lk1_ref, lq2_ref, lk2_ref, g_ref, o_ref,
                      vt_ref, acc1_ref, acc2_ref, *, tk, chunks_per_trip):
    n_chunks = vt_ref.shape[0]

    @pl.when(pl.program_id(2) == 0)
    def _transpose_values():
        row = lax.broadcasted_iota(jnp.int32, (BF16_SUBLANES, tk), 0)
        ones_tile = jnp.where(row == 0, 1.0, 0.0).astype(vt_ref.dtype)

        def body(c, carry):
            off = pl.multiple_of(c * tk, tk)
            vt_ref[c, :DIFF_DV, :] = v_ref[pl.ds(off, tk), :].astype(_F32).T.astype(vt_ref.dtype)
            vt_ref[c, DIFF_DV:, :] = ones_tile
            return carry
        lax.fori_loop(0, n_chunks, body, 0, unroll=math.gcd(n_chunks, 4))

    q = q_ref[...]
    tq = q.shape[0]
    lane = lax.broadcasted_iota(jnp.int32, q.shape, 1)
    zero = jnp.zeros_like(q)
    qa = jnp.where(lane < DIFF_DK, q, zero)
    qb = jnp.where(lane >= DIFF_DK, q, zero)
    acc_refs = (acc1_ref, acc2_ref)

    def chunk_scores(c):
        off = pl.multiple_of(c * tk, tk)
        kc = k_ref[pl.ds(off, tk), :]
        return [lax.dot_general(kc, qh, _NT, preferred_element_type=_F32) for qh in (qa, qb)]

    def weighted_values(c, s, m):
        return jnp.dot(vt_ref[c], jnp.exp2(s - m).astype(_BF16), preferred_element_type=_F32)

    def first_chunk():
        m = []
        for s, acc_ref in zip(chunk_scores(0), acc_refs):
            m.append(jnp.max(s, axis=0, keepdims=True))
            acc_ref[...] = weighted_values(0, s, m[-1])
        return tuple(m)

    def lazy_chunk(c, state):
        new_state = []
        for s, acc_ref, (top, lag) in zip(chunk_scores(c), acc_refs, state):
            m = jnp.maximum(top, 0.0)
            pv = weighted_values(c, s, m)
            chunk_max = jnp.max(s, axis=0, keepdims=True)
            acc_ref[...] = (acc_ref[...] + pv) * jnp.exp2(m - jnp.maximum(m, chunk_max))
            new_state.append((jnp.maximum(top, chunk_max), jnp.maximum(lag, chunk_max - m)))
        return tuple(new_state)

    def max_first_chunk(c, m_old):
        m_out = []
        for s, acc_ref, m in zip(chunk_scores(c), acc_refs, m_old):
            m_new = jnp.maximum(m, jnp.max(s, axis=0, keepdims=True))
            acc_ref[...] = jnp.exp2(m - m_new) * acc_ref[...] + weighted_values(c, s, m_new)
            m_out.append(m_new)
        return tuple(m_out)

    n_trips, n_tail = divmod(n_chunks, chunks_per_trip)

    def fast_body(j, state):
        for u in range(chunks_per_trip):
            state = lazy_chunk(j * chunks_per_trip + u, state)
        return state

    for acc_ref in acc_refs:
        acc_ref[...] = jnp.zeros_like(acc_ref)
    start = (jnp.full((1, tq), MASK_VALUE, _F32), jnp.zeros((1, tq), _F32))
    state = lax.fori_loop(0, n_trips, fast_body, (start, start))
    for c in range(n_chunks - n_tail, n_chunks):
        state = lazy_chunk(c, state)
    worst_lag = jnp.max(jnp.maximum(jnp.maximum(state[0][1], -state[0][0]),
                                    jnp.maximum(state[1][1], -state[1][0])))

    @pl.when(jnp.logical_not(worst_lag <= DIFF_MAX_LAG))
    def _slow_path():
        lax.fori_loop(1, n_chunks, max_first_chunk, first_chunk())

    lam = (jnp.exp(jnp.sum(lq1_ref[...] * lk1_ref[...], axis=-1, keepdims=True))
           - jnp.exp(jnp.sum(lq2_ref[...] * lk2_ref[...], axis=-1, keepdims=True)) + LAM_INIT)
    o1 = acc1_ref[:DIFF_DV, :] / acc1_ref[DIFF_DV:DIFF_DV + 1, :]
    o2 = acc2_ref[:DIFF_DV, :] / acc2_ref[DIFF_DV:DIFF_DV + 1, :]
    o = o1 - lam * o2
    inv = lax.rsqrt(jnp.mean(o * o, axis=0, keepdims=True) + SUBLN_EPS)
    o = ((o * inv) * g_ref[...]) * (1.0 - LAM_INIT)
    o_ref[...] = o.T.astype(o_ref.dtype)


def _diff_attn(proj, lam_q1, lam_k1, lam_q2, lam_k2, g_subln, *, col_q, col_k, col_v, tq, tk):
    b, t, _ = proj.shape
    assert t % tq == 0 and t % tk == 0 and t // tk >= 2
    lam_spec = pl.BlockSpec((1, DIFF_DK), lambda bi, h, i: (0, 0))
    limit = _vmem_limit(2 * _nbytes((t, LANES), _BF16), 2 * _nbytes((tq, LANES), _BF16),
                        _nbytes((t, LANES), _BF16) // 2, _nbytes((DIFF_DV, tq), _F32),
                        2 * _nbytes((tk, tq), _F32))
    return pl.pallas_call(
        functools.partial(_diff_attn_kernel, tk=tk, chunks_per_trip=DIFF_CHUNKS_PER_TRIP),
        grid=(b, DIFF_HEADS, t // tq),
        in_specs=[
            pl.BlockSpec((None, tq, LANES), lambda bi, h, i: (bi, i, col_q + h)),
            pl.BlockSpec((None, t, LANES), lambda bi, h, i: (bi, 0, col_k + h)),
            pl.BlockSpec((None, t, LANES), lambda bi, h, i: (bi, 0, col_v + h)),
            lam_spec, lam_spec, lam_spec, lam_spec,
            pl.BlockSpec((DIFF_DV, 1), lambda bi, h, i: (0, 0)),
        ],
        out_specs=pl.BlockSpec((None, tq, DIFF_DV), lambda bi, h, i: (bi, i, h)),
        out_shape=jax.ShapeDtypeStruct((b, t, DIFF_HEADS * DIFF_DV), _BF16),
        scratch_shapes=[
            pltpu.VMEM((t // tk, DIFF_ACC_ROWS, tk), _BF16),
            pltpu.VMEM((DIFF_ACC_ROWS, tq), _F32),
            pltpu.VMEM((DIFF_ACC_ROWS, tq), _F32),
        ],
        compiler_params=pltpu.CompilerParams(
            dimension_semantics=("parallel", "parallel", "arbitrary"), vmem_limit_bytes=limit),
        name="diff_attn",
    )(proj, proj, proj, lam_q1.reshape(1, DIFF_DK), lam_k1.reshape(1, DIFF_DK),
      lam_q2.reshape(1, DIFF_DK), lam_k2.reshape(1, DIFF_DK), g_subln.reshape(DIFF_DV, 1))


def _out_proj_kernel(x_ref, na_ref, df_ref, w_ref, o_ref):
    n_na = na_ref.shape[1]
    acc = jnp.dot(na_ref[...], w_ref[:n_na, :], preferred_element_type=_F32)
    acc += jnp.dot(df_ref[...], w_ref[n_na:, :], preferred_element_type=_F32)
    o_ref[...] = x_ref[...] + acc


def _out_proj(x, na_o, df_o, w_out, *, tm, tn):
    m, d = x.shape
    k_na, k_df = na_o.shape[1], df_o.shape[1]
    assert m % tm == 0 and d % tn == 0 and w_out.shape == (k_na + k_df, d)
    limit = _vmem_limit(2 * _nbytes((tm, tn), _F32), _nbytes((tm, k_na + k_df), _BF16),
                        _nbytes((k_na + k_df, tn), _BF16))
    return pl.pallas_call(
        _out_proj_kernel,
        grid=(m // tm, d // tn),
        in_specs=[
            pl.BlockSpec((tm, tn), lambda i, j: (i, j)),
            pl.BlockSpec((tm, k_na), lambda i, j: (i, 0)),
            pl.BlockSpec((tm, k_df), lambda i, j: (i, 0)),
            pl.BlockSpec((k_na + k_df, tn), lambda i, j: (0, j)),
        ],
        out_specs=pl.BlockSpec((tm, tn), lambda i, j: (i, j)),
        out_shape=jax.ShapeDtypeStruct((m, d), _F32),
        compiler_params=pltpu.CompilerParams(
            dimension_semantics=("parallel", "parallel"), vmem_limit_bytes=limit),
        name="out_proj",
    )(x, na_o, df_o, w_out)


def _mem_attn_kernel(q_ref, kv_ref, h_ref, w_ref, o_ref, ctx_ref):
    d = q_ref.shape[1]
    hd = d // MEM_HEADS
    for h in range(MEM_HEADS):
        qh = q_ref[:, h * hd:(h + 1) * hd]
        kh = kv_ref[:, h * hd:(h + 1) * hd]
        vh = kv_ref[:, d + h * hd:d + (h + 1) * hd]
        s = lax.dot_general(qh, kh, _NT, preferred_element_type=_F32) * (hd ** -0.5)
        p = jnp.exp(s - jnp.max(s, axis=-1, keepdims=True))
        l = jnp.sum(p, axis=-1, keepdims=True)
        ctx = jnp.dot(p.astype(_BF16), vh, preferred_element_type=_F32) / l
        ctx_ref[:, h * hd:(h + 1) * hd] = ctx.astype(ctx_ref.dtype)
    o_ref[...] = h_ref[...] + jnp.dot(ctx_ref[...], w_ref[...], preferred_element_type=_F32)


def _mem_attn(q, kv, h1, w_mo, *, seq_len, mem_tokens, tm):
    m, d = h1.shape
    assert m % tm == 0 and seq_len % tm == 0
    tiles_per_seq = seq_len // tm
    limit = _vmem_limit(_nbytes((tm, d), _BF16), _nbytes((mem_tokens, 2 * d), _BF16),
                        2 * _nbytes((tm, d), _F32), _nbytes((d, d), _BF16),
                        _nbytes((tm, d), _BF16) // 2)
    return pl.pallas_call(
        _mem_attn_kernel,
        grid=(m // tm,),
        in_specs=[
            pl.BlockSpec((tm, d), lambda i: (i, 0)),
            pl.BlockSpec((mem_tokens, 2 * d), lambda i: (i // tiles_per_seq, 0)),
            pl.BlockSpec((tm, d), lambda i: (i, 0)),
            pl.BlockSpec((d, d), lambda i: (0, 0)),
        ],
        out_specs=pl.BlockSpec((tm, d), lambda i: (i, 0)),
        out_shape=jax.ShapeDtypeStruct((m, d), _F32),
        scratch_shapes=[pltpu.VMEM((tm, d), _BF16)],
        compiler_params=pltpu.CompilerParams(
            dimension_semantics=("parallel",), vmem_limit_bytes=limit),
        name="mem_attn",
    )(q, kv, h1, w_mo)


def _ffn_kernel(h_ref, g_ref, wg_ref, wu_ref, wd_ref, gf_ref, o_ref, hn_ref):
    f = pl.program_id(1)

    def down_projection(hn):
        gate = jnp.dot(hn, wg_ref[...], preferred_element_type=_F32)
        up = jnp.dot(hn, wu_ref[...], preferred_element_type=_F32)
        act = (gate * jax.nn.sigmoid(gate)) * up
        return jnp.dot(act.astype(_BF16), wd_ref[...], preferred_element_type=_F32)

    @pl.when(f == 0)
    def _first():
        x = h_ref[...]
        inv = lax.rsqrt(jnp.mean(x * x, axis=-1, keepdims=True) + RMS_EPS)
        hn = ((x * inv) * g_ref[...]).astype(hn_ref.dtype)
        hn_ref[...] = hn
        o_ref[...] = x + down_projection(hn)

    @pl.when(f > 0)
    def _accumulate():
        o_ref[...] += down_projection(hn_ref[...])

    @pl.when(f == pl.num_programs(1) - 1)
    def _finish():
        y = o_ref[...]
        inv = lax.rsqrt(jnp.mean(y * y, axis=-1, keepdims=True) + RMS_EPS)
        o_ref[...] = (y * inv) * gf_ref[...]


def _ffn(h2, g_ffn, w_gate_up, w_down, g_final, *, tm, tf):
    m, d = h2.shape
    d_ff = w_down.shape[0]
    assert m % tm == 0 and d_ff % tf == 0 and w_gate_up.shape == (d, 2 * d_ff)
    n_f = d_ff // tf
    limit = _vmem_limit(2 * _nbytes((tm, d), _F32), 3 * _nbytes((d, tf), _BF16),
                        _nbytes((tm, d), _BF16) // 2)
    return pl.pallas_call(
        _ffn_kernel,
        grid=(m // tm, n_f),
        in_specs=[
            pl.BlockSpec((tm, d), lambda i, f: (i, 0)),
            pl.BlockSpec((1, d), lambda i, f: (0, 0)),
            pl.BlockSpec((d, tf), lambda i, f: (0, f)),
            pl.BlockSpec((d, tf), lambda i, f: (0, n_f + f)),
            pl.BlockSpec((tf, d), lambda i, f: (f, 0)),
            pl.BlockSpec((1, d), lambda i, f: (0, 0)),
        ],
        out_specs=pl.BlockSpec((tm, d), lambda i, f: (i, 0)),
        out_shape=jax.ShapeDtypeStruct((m, d), _F32),
        scratch_shapes=[pltpu.VMEM((tm, d), _BF16)],
        compiler_params=pltpu.CompilerParams(
            dimension_semantics=("parallel", "arbitrary"), vmem_limit_bytes=limit),
        name="ffn",
    )(h2, g_ffn.reshape(1, d), w_gate_up, w_gate_up, w_down, g_final.reshape(1, d))


def _trunk(x, mem, p, na_bias):
    b, t, d = x.shape
    mem_tokens = mem.shape[1]
    m = b * t
    x2 = x.reshape(m, d)
    na_w = NA_HEADS * NA_HD
    qk_w = DIFF_HEADS * 2 * DIFF_DK
    q_cols = (3 * na_w, 3 * na_w + qk_w)
    k_cols = (3 * na_w + qk_w, 3 * na_w + 2 * qk_w)
    proj = _in_proj(x2, p["g_mix"], p["w_in"], _rope_tables(t), seq_len=t, tm=512,
                    q_cols=q_cols, k_cols=k_cols)
    proj = proj.reshape(b, t, -1)
    blk = lambda col: col // LANES
    na_o = _na_attn(proj, na_bias, col_q=0, col_k=blk(na_w), col_v=blk(2 * na_w))
    df_o = _diff_attn(proj, p["lam_q1"], p["lam_k1"], p["lam_q2"], p["lam_k2"], p["g_subln"],
                      col_q=blk(3 * na_w), col_k=blk(3 * na_w + qk_w),
                      col_v=blk(3 * na_w + 2 * qk_w), tq=1024, tk=512)
    h1 = _out_proj(x2, na_o.reshape(m, -1), df_o.reshape(m, -1), p["w_out"], tm=512, tn=d)
    q_mem = _norm_matmul(h1, p["g_xattn"], p["w_mq"], tm=512, tn=d, name="mem_q")
    kv_mem = _norm_matmul(mem.reshape(b * mem_tokens, d), p["g_mem"], p["w_mkv"],
                          tm=min(512, b * mem_tokens), tn=512, name="mem_kv")
    h2 = _mem_attn(q_mem, kv_mem, h1, p["w_mo"], seq_len=t, mem_tokens=mem_tokens, tm=512)
    y = _ffn(h2, p["g_ffn"], p["w_gate_up"], p["w_down"], p["g_final"], tm=1024, tf=512)
    return y.reshape(b, t, d)


def kernel(x_prompt, x_sample, mem_prompt, mem_sample, g_mix, w_in, rpb, lam_q1, lam_k1,
           lam_q2, lam_k2, g_subln, w_out, g_xattn, g_mem, w_mq, w_mkv, w_mo, g_ffn,
           w_gate_up, w_down, g_final):
    assert w_in.shape[0] == 1, "single-layer trunk"
    p = dict(
        g_mix=g_mix[0], w_in=w_in[0].astype(_BF16), lam_q1=lam_q1[0], lam_k1=lam_k1[0],
        lam_q2=lam_q2[0], lam_k2=lam_k2[0], g_subln=g_subln[0], w_out=w_out[0].astype(_BF16),
        g_xattn=g_xattn[0], g_mem=g_mem[0], w_mq=w_mq[0].astype(_BF16),
        w_mkv=w_mkv[0].astype(_BF16), w_mo=w_mo[0].astype(_BF16), g_ffn=g_ffn[0],
        w_gate_up=w_gate_up[0].astype(_BF16), w_down=w_down[0].astype(_BF16), g_final=g_final,
    )
    na_bias = _na_bias(rpb[0])
    return (_trunk(x_prompt, mem_prompt, p, na_bias), _trunk(x_sample, mem_sample, p, na_bias))
```

```python
import functools
import math

import jax
import jax.numpy as jnp
import numpy as np
from jax import lax
from jax.experimental import pallas as pl
from jax.experimental.pallas import tpu as pltpu

GRID_W = 64
NA_HEADS = 8
NA_HD = 128
NA_WIN_ROWS = 8
NA_WIN_COLS = 16
DIFF_HEADS = 8
DIFF_DK = 64
DIFF_DV = 2 * DIFF_DK
ROPE_THETA = 10000.0
MEM_HEADS = 4
RMS_EPS = 1e-6
SUBLN_EPS = 1e-5
LAM_INIT = 0.8 - 0.6 * math.exp(-0.3 * 0)
LOG2_E = math.log2(math.e)
DIFF_Q_SCALE = DIFF_DK ** -0.5 * LOG2_E
BF16_SUBLANES = 16
DIFF_ACC_ROWS = DIFF_DV + BF16_SUBLANES
DIFF_CHUNKS_PER_TRIP = 16
DIFF_MAX_LAG = 64.0

LANES = 128
MXU_COLS = 256
V7X_VMEM_BYTES = 64 * 1024 * 1024
VMEM_LIMIT_CAP = V7X_VMEM_BYTES - 6 * 1024 * 1024
INTERNAL_SCRATCH_ALLOWANCE = 12 * 1024 * 1024

NA_Q_ROWS = 4
NA_K_ROWS = NA_Q_ROWS + NA_WIN_ROWS
NA_TQ = NA_Q_ROWS * GRID_W
NA_TK = NA_K_ROWS * GRID_W
NA_BLOCKS_PER_STEP = 16
ROPE_ROW_GROUP = 256
MASK_VALUE = -1e30

_NT = (((1,), (1,)), ((), ()))

_BF16 = jnp.bfloat16
_F32 = jnp.float32


def _vmem_limit(*block_bytes):
    return int(min(VMEM_LIMIT_CAP, 2 * sum(block_bytes) + INTERNAL_SCRATCH_ALLOWANCE))


def _nbytes(shape, dtype):
    return int(np.prod(shape)) * jnp.dtype(dtype).itemsize


def _resident(constant_block_index):
    return pl.Buffered(1) if constant_block_index else None


def _rms_normalize(x_ref, g_ref):
    x = x_ref[...]
    inv = lax.rsqrt(jnp.mean(x * x, axis=-1, keepdims=True) + RMS_EPS)
    return ((x * inv) * g_ref[...]).astype(_BF16)


def _norm_matmul_kernel(x_ref, g_ref, w_ref, o_ref, *scratch):
    if not scratch:
        acc = jnp.dot(_rms_normalize(x_ref, g_ref), w_ref[...], preferred_element_type=_F32)
        o_ref[...] = acc.astype(o_ref.dtype)
        return

    (xn_ref,) = scratch
    j = pl.program_id(1)

    @pl.when(j == 0)
    def _first():
        xn = _rms_normalize(x_ref, g_ref)
        xn_ref[...] = xn
        o_ref[...] = jnp.dot(xn, w_ref[...], preferred_element_type=_F32).astype(o_ref.dtype)

    @pl.when(j > 0)
    def _rest():
        acc = jnp.dot(xn_ref[...], w_ref[...], preferred_element_type=_F32)
        o_ref[...] = acc.astype(o_ref.dtype)


def _norm_matmul(x, g, w, *, tm, tn, name):
    m, d = x.shape
    n = w.shape[1]
    assert m % tm == 0 and n % tn == 0
    limit = _vmem_limit(_nbytes((tm, d), _F32), _nbytes((d, tn), _BF16), _nbytes((tm, tn), _BF16),
                        _nbytes((tm, d), _BF16) // 2, _nbytes((tm, tn), _F32))
    return pl.pallas_call(
        _norm_matmul_kernel,
        grid=(m // tm, n // tn),
        in_specs=[
            pl.BlockSpec((tm, d), lambda i, j: (i, 0)),
            pl.BlockSpec((1, d), lambda i, j: (0, 0)),
            pl.BlockSpec((d, tn), lambda i, j: (0, j), pipeline_mode=_resident(n == tn)),
        ],
        out_specs=pl.BlockSpec((tm, tn), lambda i, j: (i, j)),
        out_shape=jax.ShapeDtypeStruct((m, n), _BF16),
        scratch_shapes=[] if n == tn else [pltpu.VMEM((tm, d), _BF16)],
        compiler_params=pltpu.CompilerParams(
            dimension_semantics=("parallel", "arbitrary"), vmem_limit_bytes=limit),
        name=name,
    )(x, g.reshape(1, d), w)


def _in_proj_kernel(x_ref, g_ref, w_ref, cos_ref, sin_up_ref, sin_dn_ref, o_ref, xn_ref, *,
                    q_cols, k_cols):
    xn_ref[...] = _rms_normalize(x_ref, g_ref)
    tm, n = o_ref.shape
    rows = min(tm, ROPE_ROW_GROUP)
    for n0 in range(0, n, MXU_COLS):
        cols = slice(n0, n0 + MXU_COLS)
        is_q = q_cols[0] <= n0 < q_cols[1]
        if not (is_q or k_cols[0] <= n0 < k_cols[1]):
            acc = jnp.dot(xn_ref[...], w_ref[:, cols], preferred_element_type=_F32)
            o_ref[:, cols] = acc.astype(o_ref.dtype)
            continue
        for r0 in range(0, tm, rows):
            rs = slice(r0, r0 + rows)
            cos, sin_up, sin_dn = cos_ref[rs, :], sin_up_ref[rs, :], sin_dn_ref[rs, :]
            acc = jnp.dot(xn_ref[rs, :], w_ref[:, cols], preferred_element_type=_F32)
            for c0 in range(0, MXU_COLS, LANES):
                a = acc[:, c0:c0 + LANES]
                r = (a * cos + pltpu.roll(a, LANES - DIFF_DK // 2, 1) * sin_up
                     + pltpu.roll(a, DIFF_DK // 2, 1) * sin_dn)
                if is_q:
                    r = r * DIFF_Q_SCALE
                o_ref[rs, n0 + c0:n0 + c0 + LANES] = r.astype(o_ref.dtype)


def _in_proj(x, g, w, tables, *, seq_len, tm, q_cols, k_cols):
    m, d = x.shape
    n = w.shape[1]
    assert m % tm == 0 and seq_len % tm == 0 and n % MXU_COLS == 0
    assert all(c % MXU_COLS == 0 for c in q_cols + k_cols)
    tiles_per_seq = seq_len // tm
    table_spec = pl.BlockSpec((tm, LANES), lambda i: (i % tiles_per_seq, 0))
    limit = _vmem_limit(_nbytes((tm, d), _F32), _nbytes((d, n), _BF16) // 2,
                        _nbytes((tm, n), _BF16), 3 * _nbytes((tm, LANES), _F32),
                        _nbytes((tm, d), _BF16) // 2)
    return pl.pallas_call(
        functools.partial(_in_proj_kernel, q_cols=q_cols, k_cols=k_cols),
        grid=(m // tm,),
        in_specs=[
            pl.BlockSpec((tm, d), lambda i: (i, 0)),
            pl.BlockSpec((1, d), lambda i: (0, 0)),
            pl.BlockSpec((d, n), lambda i: (0, 0), pipeline_mode=pl.Buffered(1)),
            table_spec, table_spec, table_spec,
        ],
        out_specs=pl.BlockSpec((tm, n), lambda i: (i, 0)),
        out_shape=jax.ShapeDtypeStruct((m, n), _BF16),
        scratch_shapes=[pltpu.VMEM((tm, d), _BF16)],
        compiler_params=pltpu.CompilerParams(
            dimension_semantics=("parallel",), vmem_limit_bytes=limit),
        name="in_proj",
    )(x, g.reshape(1, d), w, *tables)


def _rope_tables(seq_len):
    inv = 1.0 / (ROPE_THETA ** (jnp.arange(0, DIFF_DK, 2, dtype=_F32) / DIFF_DK))
    ang = jnp.arange(seq_len, dtype=_F32)[:, None] * inv[None, :]
    ang = jnp.tile(ang, (1, LANES // (DIFF_DK // 2)))
    cos, sin = jnp.cos(ang), jnp.sin(ang)
    first_half = jnp.asarray((np.arange(LANES) % DIFF_DK) < DIFF_DK // 2)[None, :]
    sin_up = jnp.where(first_half, -sin, 0.0)
    sin_dn = jnp.where(first_half, 0.0, sin)
    return cos, sin_up, sin_dn


def _na_window(variant, i):
    if variant == 0:
        return 0, NA_WIN_ROWS - 1 - i
    if variant == 1:
        return i, NA_WIN_ROWS // 2 - 1 - i
    return NA_K_ROWS - NA_WIN_ROWS, -1 - i


def _na_bias_kernel(rpb_ref, o_ref):
    h = pl.program_id(0)
    n_dc = 2 * NA_WIN_COLS - 1
    c = lax.broadcasted_iota(jnp.int32, (GRID_W, LANES), 0)
    lane = lax.broadcasted_iota(jnp.int32, (GRID_W, LANES), 1)
    kc = lane & (GRID_W - 1)
    cs = jnp.clip(c - NA_WIN_COLS // 2, 0, GRID_W - NA_WIN_COLS)
    col_ok = jnp.logical_and(kc >= cs, kc < cs + NA_WIN_COLS)
    dc = kc - c + (NA_WIN_COLS - 1)
    neg = jnp.full((GRID_W, LANES), MASK_VALUE, _F32)
    by_dr = []
    for d in range(2 * NA_WIN_ROWS - 1):
        u = neg
        for n in range(n_dc):
            u = jnp.where(dc == n, rpb_ref[h, d * n_dc + n], u)
        by_dr.append(jnp.where(col_ok, u, neg))
    low_half = lane < GRID_W
    for variant in range(3):
        for i in range(NA_Q_ROWS):
            first, dr0 = _na_window(variant, i)
            for t in range(NA_K_ROWS // 2):
                halves = []
                for jj in (2 * t, 2 * t + 1):
                    valid = first <= jj < first + NA_WIN_ROWS
                    halves.append(by_dr[dr0 + jj] if valid else neg)
                tile = (halves[0] if halves[0] is halves[1]
                        else jnp.where(low_half, halves[0], halves[1]))
                o_ref[variant, i * GRID_W:(i + 1) * GRID_W, t * LANES:(t + 1) * LANES] = tile


def _na_bias(rpb):
    n_tab = (2 * NA_WIN_ROWS - 1) * (2 * NA_WIN_COLS - 1)
    return pl.pallas_call(
        _na_bias_kernel,
        grid=(NA_HEADS,),
        in_specs=[pl.BlockSpec(memory_space=pltpu.SMEM)],
        out_specs=pl.BlockSpec((None, 3, NA_TQ, NA_TK), lambda h: (h, 0, 0, 0)),
        out_shape=jax.ShapeDtypeStruct((NA_HEADS, 3, NA_TQ, NA_TK), _F32),
        compiler_params=pltpu.CompilerParams(dimension_semantics=("parallel",)),
        name="na_bias",
    )(rpb.reshape(NA_HEADS, n_tab))


def _na_attn_kernel(q_ref, k_ref, v_ref, bias_ref, o_ref, *, rows):
    last = pl.num_programs(2) * NA_BLOCKS_PER_STEP - 1
    for u in range(NA_BLOCKS_PER_STEP):
        r = pl.program_id(2) * NA_BLOCKS_PER_STEP + u
        kr0 = jnp.clip(r * NA_Q_ROWS - NA_WIN_ROWS // 2, 0, rows - NA_K_ROWS)
        start = pl.multiple_of(kr0 * GRID_W, GRID_W)
        kb = k_ref[pl.ds(start, NA_TK), :]
        vb = v_ref[pl.ds(start, NA_TK), :]
        variant = jnp.where(r == 0, 0, jnp.where(r == last, 2, 1))
        q = q_ref[u * NA_TQ:(u + 1) * NA_TQ, :]
        s = lax.dot_general(q, kb, _NT, preferred_element_type=_F32)
        s = s * (NA_HD ** -0.5) + bias_ref[variant]
        p = jnp.exp(s - jnp.max(s, axis=-1, keepdims=True))
        l = jnp.sum(p, axis=-1, keepdims=True)
        o = jnp.dot(p.astype(_BF16), vb, preferred_element_type=_F32) / l
        o_ref[u * NA_TQ:(u + 1) * NA_TQ, :] = o.astype(o_ref.dtype)


def _na_attn(proj, bias, *, col_q, col_k, col_v):
    b, t, _ = proj.shape
    rows = t // GRID_W
    tq = NA_BLOCKS_PER_STEP * NA_TQ
    assert t % tq == 0 and rows >= NA_K_ROWS + NA_Q_ROWS
    limit = _vmem_limit(2 * _nbytes((t, NA_HD), _BF16), _nbytes((3, NA_TQ, NA_TK), _F32),
                        2 * _nbytes((tq, NA_HD), _BF16),
                        2 * NA_BLOCKS_PER_STEP * _nbytes((NA_TQ, NA_TK), _F32))
    return pl.pallas_call(
        functools.partial(_na_attn_kernel, rows=rows),
        grid=(b, NA_HEADS, t // tq),
        in_specs=[
            pl.BlockSpec((None, tq, NA_HD), lambda bi, h, r: (bi, r, col_q + h)),
            pl.BlockSpec((None, t, NA_HD), lambda bi, h, r: (bi, 0, col_k + h)),
            pl.BlockSpec((None, t, NA_HD), lambda bi, h, r: (bi, 0, col_v + h)),
            pl.BlockSpec((None, 3, NA_TQ, NA_TK), lambda bi, h, r: (h, 0, 0, 0)),
        ],
        out_specs=pl.BlockSpec((None, tq, NA_HD), lambda bi, h, r: (bi, r, h)),
        out_shape=jax.ShapeDtypeStruct((b, t, NA_HEADS * NA_HD), _BF16),
        compiler_params=pltpu.CompilerParams(
            dimension_semantics=("parallel", "parallel", "arbitrary"), vmem_limit_bytes=limit),
        name="na_attn",
    )(proj, proj, proj, bias)


def _diff_attn_kernel(q_ref, k_ref, v_ref, lq1_ref, lk1_ref, lq2_ref, lk2_ref, g_ref, o_ref,
                      vt_ref, acc1_ref, acc2_ref, *, tk, chunks_per_trip):
    n_chunks = vt_ref.shape[0]

    @pl.when(pl.program_id(2) == 0)
    def _transpose_values():
        row = lax.broadcasted_iota(jnp.int32, (BF16_SUBLANES, tk), 0)
        ones_tile = jnp.where(row == 0, 1.0, 0.0).astype(vt_ref.dtype)

        def body(c, carry):
            off = pl.multiple_of(c * tk, tk)
            vt_ref[c, :DIFF_DV, :] = v_ref[pl.ds(off, tk), :].astype(_F32).T.astype(vt_ref.dtype)
            vt_ref[c, DIFF_DV:, :] = ones_tile
            return carry
        lax.fori_loop(0, n_chunks, body, 0, unroll=math.gcd(n_chunks, 4))

    q = q_ref[...]
    tq = q.shape[0]
    lane = lax.broadcasted_iota(jnp.int32, q.shape, 1)
    zero = jnp.zeros_like(q)
    qa = jnp.where(lane < DIFF_DK, q, zero)
    qb = jnp.where(lane >= DIFF_DK, q, zero)
    acc_refs = (acc1_ref, acc2_ref)

    def chunk_scores(c):
        off = pl.multiple_of(c * tk, tk)
        kc = k_ref[pl.ds(off, tk), :]
        return [lax.dot_general(kc, qh, _NT, preferred_element_type=_F32) for qh in (qa, qb)]

    def weighted_values(c, s, m):
        return jnp.dot(vt_ref[c], jnp.exp2(s - m).astype(_BF16), preferred_element_type=_F32)

    def first_chunk():
        m = []
        for s, acc_ref in zip(chunk_scores(0), acc_refs):
            m.append(jnp.max(s, axis=0, keepdims=True))
            acc_ref[...] = weighted_values(0, s, m[-1])
        return tuple(m)

    def lazy_chunk(c, state):
        new_state = []
        for s, acc_ref, (top, lag) in zip(chunk_scores(c), acc_refs, state):
            m = jnp.maximum(top, 0.0)
            pv = weighted_values(c, s, m)
            chunk_max = jnp.max(s, axis=0, keepdims=True)
            acc_ref[...] = (acc_ref[...] + pv) * jnp.exp2(m - jnp.maximum(m, chunk_max))
            new_state.append((jnp.maximum(top, chunk_max), jnp.maximum(lag, chunk_max - m)))
        return tuple(new_state)

    def max_first_chunk(c, m_old):
        m_out = []
        for s, acc_ref, m in zip(chunk_scores(c), acc_refs, m_old):
            m_new = jnp.maximum(m, jnp.max(s, axis=0, keepdims=True))
            acc_ref[...] = jnp.exp2(m - m_new) * acc_ref[...] + weighted_values(c, s, m_new)
            m_out.append(m_new)
        return tuple(m_out)

    n_trips, n_tail = divmod(n_chunks, chunks_per_trip)

    def fast_body(j, state):
        for u in range(chunks_per_trip):
            state = lazy_chunk(j * chunks_per_trip + u, state)
        return state

    for acc_ref in acc_refs:
        acc_ref[...] = jnp.zeros_like(acc_ref)
    start = (jnp.full((1, tq), MASK_VALUE, _F32), jnp.zeros((1, tq), _F32))
    state = lax.fori_loop(0, n_trips, fast_body, (start, start))
    for c in range(n_chunks - n_tail, n_chunks):
        state = lazy_chunk(c, state)
    worst_lag = jnp.max(jnp.maximum(jnp.maximum(state[0][1], -state[0][0]),
                                    jnp.maximum(state[1][1], -state[1][0])))

    @pl.when(jnp.logical_not(worst_lag <= DIFF_MAX_LAG))
    def _slow_path():
        lax.fori_loop(1, n_chunks, max_first_chunk, first_chunk())

    lam = (jnp.exp(jnp.sum(lq1_ref[...] * lk1_ref[...], axis=-1, keepdims=True))
           - jnp.exp(jnp.sum(lq2_ref[...] * lk2_ref[...], axis=-1, keepdims=True)) + LAM_INIT)
    o1 = acc1_ref[:DIFF_DV, :] / acc1_ref[DIFF_DV:DIFF_DV + 1, :]
    o2 = acc2_ref[:DIFF_DV, :] / acc2_ref[DIFF_DV:DIFF_DV + 1, :]
    o = o1 - lam * o2
    inv = lax.rsqrt(jnp.mean(o * o, axis=0, keepdims=True) + SUBLN_EPS)
    o = ((o * inv) * g_ref[...]) * (1.0 - LAM_INIT)
    o_ref[...] = o.T.astype(o_ref.dtype)


def _diff_attn(proj, lam_q1, lam_k1, lam_q2, lam_k2, g_subln, *, col_q, col_k, col_v, tq, tk):
    b, t, _ = proj.shape
    assert t % tq == 0 and t % tk == 0 and t // tk >= 2
    lam_spec = pl.BlockSpec((1, DIFF_DK), lambda bi, h, i: (0, 0))
    limit = _vmem_limit(2 * _nbytes((t, LANES), _BF16), 2 * _nbytes((tq, LANES), _BF16),
                        _nbytes((t, LANES), _BF16) // 2, _nbytes((DIFF_DV, tq), _F32),
                        2 * _nbytes((tk, tq), _F32))
    return pl.pallas_call(
        functools.partial(_diff_attn_kernel, tk=tk, chunks_per_trip=DIFF_CHUNKS_PER_TRIP),
        grid=(b, DIFF_HEADS, t // tq),
        in_specs=[
            pl.BlockSpec((None, tq, LANES), lambda bi, h, i: (bi, i, col_q + h)),
            pl.BlockSpec((None, t, LANES), lambda bi, h, i: (bi, 0, col_k + h)),
            pl.BlockSpec((None, t, LANES), lambda bi, h, i: (bi, 0, col_v + h)),
            lam_spec, lam_spec, lam_spec, lam_spec,
            pl.BlockSpec((DIFF_DV, 1), lambda bi, h, i: (0, 0)),
        ],
        out_specs=pl.BlockSpec((None, tq, DIFF_DV), lambda bi, h, i: (bi, i, h)),
        out_shape=jax.ShapeDtypeStruct((b, t, DIFF_HEADS * DIFF_DV), _BF16),
        scratch_shapes=[
            pltpu.VMEM((t // tk, DIFF_ACC_ROWS, tk), _BF16),
            pltpu.VMEM((DIFF_ACC_ROWS, tq), _F32),
            pltpu.VMEM((DIFF_ACC_ROWS, tq), _F32),
        ],
        compiler_params=pltpu.CompilerParams(
            dimension_semantics=("parallel", "parallel", "arbitrary"), vmem_limit_bytes=limit),
        name="diff_attn",
    )(proj, proj, proj, lam_q1.reshape(1, DIFF_DK), lam_k1.reshape(1, DIFF_DK),
      lam_q2.reshape(1, DIFF_DK), lam_k2.reshape(1, DIFF_DK), g_subln.reshape(DIFF_DV, 1))


def _out_proj_kernel(x_ref, na_ref, df_ref, w_ref, o_ref):
    n_na = na_ref.shape[1]
    acc = jnp.dot(na_ref[...], w_ref[:n_na, :], preferred_element_type=_F32)
    acc += jnp.dot(df_ref[...], w_ref[n_na:, :], preferred_element_type=_F32)
    o_ref[...] = x_ref[...] + acc


def _out_proj(x, na_o, df_o, w_out, *, tm, tn):
    m, d = x.shape
    k_na, k_df = na_o.shape[1], df_o.shape[1]
    assert m % tm == 0 and d % tn == 0 and w_out.shape == (k_na + k_df, d)
    limit = _vmem_limit(2 * _nbytes((tm, tn), _F32), _nbytes((tm, k_na + k_df), _BF16),
                        _nbytes((k_na + k_df, tn), _BF16))
    return pl.pallas_call(
        _out_proj_kernel,
        grid=(m // tm, d // tn),
        in_specs=[
            pl.BlockSpec((tm, tn), lambda i, j: (i, j)),
            pl.BlockSpec((tm, k_na), lambda i, j: (i, 0)),
            pl.BlockSpec((tm, k_df), lambda i, j: (i, 0)),
            pl.BlockSpec((k_na + k_df, tn), lambda i, j: (0, j), pipeline_mode=_resident(d == tn)),
        ],
        out_specs=pl.BlockSpec((tm, tn), lambda i, j: (i, j)),
        out_shape=jax.ShapeDtypeStruct((m, d), _F32),
        compiler_params=pltpu.CompilerParams(
            dimension_semantics=("parallel", "parallel"), vmem_limit_bytes=limit),
        name="out_proj",
    )(x, na_o, df_o, w_out)


def _mem_attn_kernel(q_ref, kv_ref, h_ref, w_ref, o_ref, ctx_ref):
    d = q_ref.shape[1]
    hd = d // MEM_HEADS
    for h in range(MEM_HEADS):
        qh = q_ref[:, h * hd:(h + 1) * hd]
        kh = kv_ref[:, h * hd:(h + 1) * hd]
        vh = kv_ref[:, d + h * hd:d + (h + 1) * hd]
        s = lax.dot_general(qh, kh, _NT, preferred_element_type=_F32) * (hd ** -0.5)
        p = jnp.exp(s - jnp.max(s, axis=-1, keepdims=True))
        l = jnp.sum(p, axis=-1, keepdims=True)
        ctx = jnp.dot(p.astype(_BF16), vh, preferred_element_type=_F32) / l
        ctx_ref[:, h * hd:(h + 1) * hd] = ctx.astype(ctx_ref.dtype)
    o_ref[...] = h_ref[...] + jnp.dot(ctx_ref[...], w_ref[...], preferred_element_type=_F32)


def _mem_attn(q, kv, h1, w_mo, *, seq_len, mem_tokens, tm):
    m, d = h1.shape
    assert m % tm == 0 and seq_len % tm == 0
    tiles_per_seq = seq_len // tm
    limit = _vmem_limit(_nbytes((tm, d), _BF16), _nbytes((mem_tokens, 2 * d), _BF16),
                        2 * _nbytes((tm, d), _F32), _nbytes((d, d), _BF16),
                        _nbytes((tm, d), _BF16) // 2)
    return pl.pallas_call(
        _mem_attn_kernel,
        grid=(m // tm,),
        in_specs=[
            pl.BlockSpec((tm, d), lambda i: (i, 0)),
            pl.BlockSpec((mem_tokens, 2 * d), lambda i: (i // tiles_per_seq, 0)),
            pl.BlockSpec((tm, d), lambda i: (i, 0)),
            pl.BlockSpec((d, d), lambda i: (0, 0)),
        ],
        out_specs=pl.BlockSpec((tm, d), lambda i: (i, 0)),
        out_shape=jax.ShapeDtypeStruct((m, d), _F32),
        scratch_shapes=[pltpu.VMEM((tm, d), _BF16)],
        compiler_params=pltpu.CompilerParams(
            dimension_semantics=("parallel",), vmem_limit_bytes=limit),
        name="mem_attn",
    )(q, kv, h1, w_mo)


def _ffn_kernel(h_ref, g_ref, wg_ref, wu_ref, wd_ref, gf_ref, o_ref, hn_ref):
    f = pl.program_id(1)

    def down_projection(hn):
        gate = jnp.dot(hn, wg_ref[...], preferred_element_type=_F32)
        up = jnp.dot(hn, wu_ref[...], preferred_element_type=_F32)
        act = (gate * jax.nn.sigmoid(gate)) * up
        return jnp.dot(act.astype(_BF16), wd_ref[...], preferred_element_type=_F32)

    @pl.when(f == 0)
    def _first():
        x = h_ref[...]
        inv = lax.rsqrt(jnp.mean(x * x, axis=-1, keepdims=True) + RMS_EPS)
        hn = ((x * inv) * g_ref[...]).astype(hn_ref.dtype)
        hn_ref[...] = hn
        o_ref[...] = x + down_projection(hn)

    @pl.when(f > 0)
    def _accumulate():
        o_ref[...] += down_projection(hn_ref[...])

    @pl.when(f == pl.num_programs(1) - 1)
    def _finish():
        y = o_ref[...]
        inv = lax.rsqrt(jnp.mean(y * y, axis=-1, keepdims=True) + RMS_EPS)
        o_ref[...] = (y * inv) * gf_ref[...]


def _ffn(h2, g_ffn, w_gate_up, w_down, g_final, *, tm, tf):
    m, d = h2.shape
    d_ff = w_down.shape[0]
    assert m % tm == 0 and d_ff % tf == 0 and w_gate_up.shape == (d, 2 * d_ff)
    n_f = d_ff // tf
    limit = _vmem_limit(2 * _nbytes((tm, d), _F32), 3 * _nbytes((d, tf), _BF16),
                        _nbytes((tm, d), _BF16) // 2)
    return pl.pallas_call(
        _ffn_kernel,
        grid=(m // tm, n_f),
        in_specs=[
            pl.BlockSpec((tm, d), lambda i, f: (i, 0)),
            pl.BlockSpec((1, d), lambda i, f: (0, 0)),
            pl.BlockSpec((d, tf), lambda i, f: (0, f)),
            pl.BlockSpec((d, tf), lambda i, f: (0, n_f + f)),
            pl.BlockSpec((tf, d), lambda i, f: (f, 0)),
            pl.BlockSpec((1, d), lambda i, f: (0, 0)),
        ],
        out_specs=pl.BlockSpec((tm, d), lambda i, f: (i, 0)),
        out_shape=jax.ShapeDtypeStruct((m, d), _F32),
        scratch_shapes=[pltpu.VMEM((tm, d), _BF16)],
        compiler_params=pltpu.CompilerParams(
            dimension_semantics=("parallel", "arbitrary"), vmem_limit_bytes=limit),
        name="ffn",
    )(h2, g_ffn.reshape(1, d), w_gate_up, w_gate_up, w_down, g_final.reshape(1, d))


def _trunk(x, mem, p, na_bias):
    b, t, d = x.shape
    mem_tokens = mem.shape[1]
    m = b * t
    x2 = x.reshape(m, d)
    na_w = NA_HEADS * NA_HD
    qk_w = DIFF_HEADS * 2 * DIFF_DK
    q_cols = (3 * na_w, 3 * na_w + qk_w)
    k_cols = (3 * na_w + qk_w, 3 * na_w + 2 * qk_w)
    proj = _in_proj(x2, p["g_mix"], p["w_in"], _rope_tables(t), seq_len=t, tm=512,
                    q_cols=q_cols, k_cols=k_cols)
    proj = proj.reshape(b, t, -1)
    blk = lambda col: col // LANES
    na_o = _na_attn(proj, na_bias, col_q=0, col_k=blk(na_w), col_v=blk(2 * na_w))
    df_o = _diff_attn(proj, p["lam_q1"], p["lam_k1"], p["lam_q2"], p["lam_k2"], p["g_subln"],
                      col_q=blk(3 * na_w), col_k=blk(3 * na_w + qk_w),
                      col_v=blk(3 * na_w + 2 * qk_w), tq=1024, tk=512)
    h1 = _out_proj(x2, na_o.reshape(m, -1), df_o.reshape(m, -1), p["w_out"], tm=1024, tn=d)
    q_mem = _norm_matmul(h1, p["g_xattn"], p["w_mq"], tm=1024, tn=d, name="mem_q")
    kv_mem = _norm_matmul(mem.reshape(b * mem_tokens, d), p["g_mem"], p["w_mkv"],
                          tm=min(512, b * mem_tokens), tn=512, name="mem_kv")
    h2 = _mem_attn(q_mem, kv_mem, h1, p["w_mo"], seq_len=t, mem_tokens=mem_tokens, tm=512)
    y = _ffn(h2, p["g_ffn"], p["w_gate_up"], p["w_down"], p["g_final"], tm=1024, tf=512)
    return y.reshape(b, t, d)


def kernel(x_prompt, x_sample, mem_prompt, mem_sample, g_mix, w_in, rpb, lam_q1, lam_k1,
           lam_q2, lam_k2, g_subln, w_out, g_xattn, g_mem, w_mq, w_mkv, w_mo, g_ffn,
           w_gate_up, w_down, g_final):
    assert w_in.shape[0] == 1, "single-layer trunk"
    p = dict(
        g_mix=g_mix[0], w_in=w_in[0].astype(_BF16), lam_q1=lam_q1[0], lam_k1=lam_k1[0],
        lam_q2=lam_q2[0], lam_k2=lam_k2[0], g_subln=g_subln[0], w_out=w_out[0].astype(_BF16),
        g_xattn=g_xattn[0], g_mem=g_mem[0], w_mq=w_mq[0].astype(_BF16),
        w_mkv=w_mkv[0].astype(_BF16), w_mo=w_mo[0].astype(_BF16), g_ffn=g_ffn[0],
        w_gate_up=w_gate_up[0].astype(_BF16), w_down=w_down[0].astype(_BF16), g_final=g_final,
    )
    na_bias = _na_bias(rpb[0])
    return (_trunk(x_prompt, mem_prompt, p, na_bias), _trunk(x_sample, mem_sample, p, na_bias))
```

```python
import functools
import math

import jax
import jax.numpy as jnp
import numpy as np
from jax import lax
from jax.experimental import pallas as pl
from jax.experimental.pallas import tpu as pltpu

GRID_W = 64
NA_HEADS = 8
NA_HD = 128
NA_WIN_ROWS = 8
NA_WIN_COLS = 16
DIFF_HEADS = 8
DIFF_DK = 64
DIFF_DV = 2 * DIFF_DK
ROPE_THETA = 10000.0
MEM_HEADS = 4
RMS_EPS = 1e-6
SUBLN_EPS = 1e-5
LAM_INIT = 0.8 - 0.6 * math.exp(-0.3 * 0)
LOG2_E = math.log2(math.e)
DIFF_Q_SCALE = DIFF_DK ** -0.5 * LOG2_E
BF16_SUBLANES = 16
DIFF_ACC_ROWS = DIFF_DV + BF16_SUBLANES
DIFF_CHUNKS_PER_TRIP = 16
DIFF_MAX_LAG = 64.0

LANES = 128
MXU_COLS = 256
V7X_VMEM_BYTES = 64 * 1024 * 1024
VMEM_LIMIT_CAP = V7X_VMEM_BYTES - 6 * 1024 * 1024
INTERNAL_SCRATCH_ALLOWANCE = 12 * 1024 * 1024

NA_Q_ROWS = 4
NA_K_ROWS = NA_Q_ROWS + NA_WIN_ROWS
NA_TQ = NA_Q_ROWS * GRID_W
NA_TK = NA_K_ROWS * GRID_W
NA_BLOCKS_PER_STEP = 16
ROPE_ROW_GROUP = 256
MASK_VALUE = -1e30

_NT = (((1,), (1,)), ((), ()))

_BF16 = jnp.bfloat16
_F32 = jnp.float32


def _vmem_limit(*block_bytes):
    return int(min(VMEM_LIMIT_CAP, 2 * sum(block_bytes) + INTERNAL_SCRATCH_ALLOWANCE))


def _nbytes(shape, dtype):
    return int(np.prod(shape)) * jnp.dtype(dtype).itemsize


def _resident(constant_block_index):
    return pl.Buffered(1) if constant_block_index else None


def _rms_normalize(x_ref, g_ref):
    x = x_ref[...]
    inv = lax.rsqrt(jnp.mean(x * x, axis=-1, keepdims=True) + RMS_EPS)
    return ((x * inv) * g_ref[...]).astype(_BF16)


def _norm_matmul_kernel(x_ref, g_ref, w_ref, o_ref, *scratch):
    if not scratch:
        acc = jnp.dot(_rms_normalize(x_ref, g_ref), w_ref[...], preferred_element_type=_F32)
        o_ref[...] = acc.astype(o_ref.dtype)
        return

    (xn_ref,) = scratch
    j = pl.program_id(1)

    @pl.when(j == 0)
    def _first():
        xn = _rms_normalize(x_ref, g_ref)
        xn_ref[...] = xn
        o_ref[...] = jnp.dot(xn, w_ref[...], preferred_element_type=_F32).astype(o_ref.dtype)

    @pl.when(j > 0)
    def _rest():
        acc = jnp.dot(xn_ref[...], w_ref[...], preferred_element_type=_F32)
        o_ref[...] = acc.astype(o_ref.dtype)


def _norm_matmul(x, g, w, *, tm, tn, name):
    m, d = x.shape
    n = w.shape[1]
    assert m % tm == 0 and n % tn == 0
    limit = _vmem_limit(_nbytes((tm, d), _F32), _nbytes((d, tn), _BF16), _nbytes((tm, tn), _BF16),
                        _nbytes((tm, d), _BF16) // 2, _nbytes((tm, tn), _F32))
    return pl.pallas_call(
        _norm_matmul_kernel,
        grid=(m // tm, n // tn),
        in_specs=[
            pl.BlockSpec((tm, d), lambda i, j: (i, 0)),
            pl.BlockSpec((1, d), lambda i, j: (0, 0)),
            pl.BlockSpec((d, tn), lambda i, j: (0, j), pipeline_mode=_resident(n == tn)),
        ],
        out_specs=pl.BlockSpec((tm, tn), lambda i, j: (i, j)),
        out_shape=jax.ShapeDtypeStruct((m, n), _BF16),
        scratch_shapes=[] if n == tn else [pltpu.VMEM((tm, d), _BF16)],
        compiler_params=pltpu.CompilerParams(
            dimension_semantics=("parallel", "arbitrary"), vmem_limit_bytes=limit),
        name=name,
    )(x, g.reshape(1, d), w)


def _in_proj_kernel(x_ref, g_ref, w_ref, cos_ref, sin_up_ref, sin_dn_ref, o_ref, xn_ref, *,
                    q_cols, k_cols):
    xn_ref[...] = _rms_normalize(x_ref, g_ref)
    tm, n = o_ref.shape
    rows = min(tm, ROPE_ROW_GROUP)
    for n0 in range(0, n, MXU_COLS):
        cols = slice(n0, n0 + MXU_COLS)
        is_q = q_cols[0] <= n0 < q_cols[1]
        if not (is_q or k_cols[0] <= n0 < k_cols[1]):
            acc = jnp.dot(xn_ref[...], w_ref[:, cols], preferred_element_type=_F32)
            o_ref[:, cols] = acc.astype(o_ref.dtype)
            continue
        for r0 in range(0, tm, rows):
            rs = slice(r0, r0 + rows)
            cos, sin_up, sin_dn = cos_ref[rs, :], sin_up_ref[rs, :], sin_dn_ref[rs, :]
            acc = jnp.dot(xn_ref[rs, :], w_ref[:, cols], preferred_element_type=_F32)
            for c0 in range(0, MXU_COLS, LANES):
                a = acc[:, c0:c0 + LANES]
                r = (a * cos + pltpu.roll(a, LANES - DIFF_DK // 2, 1) * sin_up
                     + pltpu.roll(a, DIFF_DK // 2, 1) * sin_dn)
                if is_q:
                    r = r * DIFF_Q_SCALE
                o_ref[rs, n0 + c0:n0 + c0 + LANES] = r.astype(o_ref.dtype)


def _in_proj(x, g, w, tables, *, seq_len, tm, q_cols, k_cols):
    m, d = x.shape
    n = w.shape[1]
    assert m % tm == 0 and seq_len % tm == 0 and n % MXU_COLS == 0
    assert all(c % MXU_COLS == 0 for c in q_cols + k_cols)
    tiles_per_seq = seq_len // tm
    table_spec = pl.BlockSpec((tm, LANES), lambda i: (i % tiles_per_seq, 0))
    limit = _vmem_limit(_nbytes((tm, d), _F32), _nbytes((d, n), _BF16) // 2,
                        _nbytes((tm, n), _BF16), 3 * _nbytes((tm, LANES), _F32),
                        _nbytes((tm, d), _BF16) // 2)
    return pl.pallas_call(
        functools.partial(_in_proj_kernel, q_cols=q_cols, k_cols=k_cols),
        grid=(m // tm,),
        in_specs=[
            pl.BlockSpec((tm, d), lambda i: (i, 0)),
            pl.BlockSpec((1, d), lambda i: (0, 0)),
            pl.BlockSpec((d, n), lambda i: (0, 0), pipeline_mode=pl.Buffered(1)),
            table_spec, table_spec, table_spec,
        ],
        out_specs=pl.BlockSpec((tm, n), lambda i: (i, 0)),
        out_shape=jax.ShapeDtypeStruct((m, n), _BF16),
        scratch_shapes=[pltpu.VMEM((tm, d), _BF16)],
        compiler_params=pltpu.CompilerParams(
            dimension_semantics=("parallel",), vmem_limit_bytes=limit),
        name="in_proj",
    )(x, g.reshape(1, d), w, *tables)


def _rope_tables(seq_len):
    inv = 1.0 / (ROPE_THETA ** (jnp.arange(0, DIFF_DK, 2, dtype=_F32) / DIFF_DK))
    ang = jnp.arange(seq_len, dtype=_F32)[:, None] * inv[None, :]
    ang = jnp.tile(ang, (1, LANES // (DIFF_DK // 2)))
    cos, sin = jnp.cos(ang), jnp.sin(ang)
    first_half = jnp.asarray((np.arange(LANES) % DIFF_DK) < DIFF_DK // 2)[None, :]
    sin_up = jnp.where(first_half, -sin, 0.0)
    sin_dn = jnp.where(first_half, 0.0, sin)
    return cos, sin_up, sin_dn


def _na_window(variant, i):
    if variant == 0:
        return 0, NA_WIN_ROWS - 1 - i
    if variant == 1:
        return i, NA_WIN_ROWS // 2 - 1 - i
    return NA_K_ROWS - NA_WIN_ROWS, -1 - i


def _na_bias_kernel(rpb_ref, o_ref):
    h = pl.program_id(0)
    n_dc = 2 * NA_WIN_COLS - 1
    c = lax.broadcasted_iota(jnp.int32, (GRID_W, LANES), 0)
    lane = lax.broadcasted_iota(jnp.int32, (GRID_W, LANES), 1)
    kc = lane & (GRID_W - 1)
    cs = jnp.clip(c - NA_WIN_COLS // 2, 0, GRID_W - NA_WIN_COLS)
    col_ok = jnp.logical_and(kc >= cs, kc < cs + NA_WIN_COLS)
    dc = kc - c + (NA_WIN_COLS - 1)
    neg = jnp.full((GRID_W, LANES), MASK_VALUE, _F32)
    by_dr = []
    for d in range(2 * NA_WIN_ROWS - 1):
        u = neg
        for n in range(n_dc):
            u = jnp.where(dc == n, rpb_ref[h, d * n_dc + n], u)
        by_dr.append(jnp.where(col_ok, u, neg))
    low_half = lane < GRID_W
    for variant in range(3):
        for i in range(NA_Q_ROWS):
            first, dr0 = _na_window(variant, i)
            for t in range(NA_K_ROWS // 2):
                halves = []
                for jj in (2 * t, 2 * t + 1):
                    valid = first <= jj < first + NA_WIN_ROWS
                    halves.append(by_dr[dr0 + jj] if valid else neg)
                tile = (halves[0] if halves[0] is halves[1]
                        else jnp.where(low_half, halves[0], halves[1]))
                o_ref[variant, i * GRID_W:(i + 1) * GRID_W, t * LANES:(t + 1) * LANES] = tile


def _na_bias(rpb):
    n_tab = (2 * NA_WIN_ROWS - 1) * (2 * NA_WIN_COLS - 1)
    return pl.pallas_call(
        _na_bias_kernel,
        grid=(NA_HEADS,),
        in_specs=[pl.BlockSpec(memory_space=pltpu.SMEM)],
        out_specs=pl.BlockSpec((None, 3, NA_TQ, NA_TK), lambda h: (h, 0, 0, 0)),
        out_shape=jax.ShapeDtypeStruct((NA_HEADS, 3, NA_TQ, NA_TK), _F32),
        compiler_params=pltpu.CompilerParams(dimension_semantics=("parallel",)),
        name="na_bias",
    )(rpb.reshape(NA_HEADS, n_tab))


def _na_attn_kernel(q_ref, k_ref, v_ref, bias_ref, o_ref, *, rows):
    last = pl.num_programs(2) * NA_BLOCKS_PER_STEP - 1
    for u in range(NA_BLOCKS_PER_STEP):
        r = pl.program_id(2) * NA_BLOCKS_PER_STEP + u
        kr0 = jnp.clip(r * NA_Q_ROWS - NA_WIN_ROWS // 2, 0, rows - NA_K_ROWS)
        start = pl.multiple_of(kr0 * GRID_W, GRID_W)
        kb = k_ref[pl.ds(start, NA_TK), :]
        vb = v_ref[pl.ds(start, NA_TK), :]
        variant = jnp.where(r == 0, 0, jnp.where(r == last, 2, 1))
        q = q_ref[u * NA_TQ:(u + 1) * NA_TQ, :]
        s = lax.dot_general(q, kb, _NT, preferred_element_type=_F32)
        s = s * (NA_HD ** -0.5) + bias_ref[variant]
        p = jnp.exp(s - jnp.max(s, axis=-1, keepdims=True))
        l = jnp.sum(p, axis=-1, keepdims=True)
        o = jnp.dot(p.astype(_BF16), vb, preferred_element_type=_F32) / l
        o_ref[u * NA_TQ:(u + 1) * NA_TQ, :] = o.astype(o_ref.dtype)


def _na_attn(proj, bias, *, col_q, col_k, col_v):
    b, t, _ = proj.shape
    rows = t // GRID_W
    tq = NA_BLOCKS_PER_STEP * NA_TQ
    assert t % tq == 0 and rows >= NA_K_ROWS + NA_Q_ROWS
    limit = _vmem_limit(2 * _nbytes((t, NA_HD), _BF16), _nbytes((3, NA_TQ, NA_TK), _F32),
                        2 * _nbytes((tq, NA_HD), _BF16),
                        2 * NA_BLOCKS_PER_STEP * _nbytes((NA_TQ, NA_TK), _F32))
    return pl.pallas_call(
        functools.partial(_na_attn_kernel, rows=rows),
        grid=(b, NA_HEADS, t // tq),
        in_specs=[
            pl.BlockSpec((None, tq, NA_HD), lambda bi, h, r: (bi, r, col_q + h)),
            pl.BlockSpec((None, t, NA_HD), lambda bi, h, r: (bi, 0, col_k + h)),
            pl.BlockSpec((None, t, NA_HD), lambda bi, h, r: (bi, 0, col_v + h)),
            pl.BlockSpec((None, 3, NA_TQ, NA_TK), lambda bi, h, r: (h, 0, 0, 0)),
        ],
        out_specs=pl.BlockSpec((None, tq, NA_HD), lambda bi, h, r: (bi, r, h)),
        out_shape=jax.ShapeDtypeStruct((b, t, NA_HEADS * NA_HD), _BF16),
        compiler_params=pltpu.CompilerParams(
            dimension_semantics=("parallel", "parallel", "arbitrary"), vmem_limit_bytes=limit),
        name="na_attn",
    )(proj, proj, proj, bias)


def _diff_attn_kernel(q_ref, k_ref, v_ref, lq1_ref, lk1_ref, lq2_ref, lk2_ref, g_ref, o_ref,
                      vt_ref, acc1_ref, acc2_ref, *, tk, chunks_per_trip):
    n_chunks = vt_ref.shape[0]

    @pl.when(pl.program_id(2) == 0)
    def _transpose_values():
        row = lax.broadcasted_iota(jnp.int32, (BF16_SUBLANES, tk), 0)
        ones_tile = jnp.where(row == 0, 1.0, 0.0).astype(vt_ref.dtype)

        def body(c, carry):
            off = pl.multiple_of(c * tk, tk)
            vt_ref[c, :DIFF_DV, :] = v_ref[pl.ds(off, tk), :].astype(_F32).T.astype(vt_ref.dtype)
            vt_ref[c, DIFF_DV:, :] = ones_tile
            return carry
        lax.fori_loop(0, n_chunks, body, 0, unroll=math.gcd(n_chunks, 4))

    q = q_ref[...]
    tq = q.shape[0]
    lane = lax.broadcasted_iota(jnp.int32, q.shape, 1)
    zero = jnp.zeros_like(q)
    qa = jnp.where(lane < DIFF_DK, q, zero)
    qb = jnp.where(lane >= DIFF_DK, q, zero)
    acc_refs = (acc1_ref, acc2_ref)

    def chunk_scores(c):
        off = pl.multiple_of(c * tk, tk)
        kc = k_ref[pl.ds(off, tk), :]
        return [lax.dot_general(kc, qh, _NT, preferred_element_type=_F32) for qh in (qa, qb)]

    def weighted_values(c, s, m):
        return jnp.dot(vt_ref[c], jnp.exp2(s - m).astype(_BF16), preferred_element_type=_F32)

    def first_chunk():
        m = []
        for s, acc_ref in zip(chunk_scores(0), acc_refs):
            m.append(jnp.max(s, axis=0, keepdims=True))
            acc_ref[...] = weighted_values(0, s, m[-1])
        return tuple(m)

    def lazy_chunk(c, state):
        new_state = []
        for s, acc_ref, (top, lag) in zip(chunk_scores(c), acc_refs, state):
            m = jnp.maximum(top, 0.0)
            pv = weighted_values(c, s, m)
            chunk_max = jnp.max(s, axis=0, keepdims=True)
            acc_ref[...] = (acc_ref[...] + pv) * jnp.exp2(m - jnp.maximum(m, chunk_max))
            new_state.append((jnp.maximum(top, chunk_max), jnp.maximum(lag, chunk_max - m)))
        return tuple(new_state)

    def max_first_chunk(c, m_old):
        m_out = []
        for s, acc_ref, m in zip(chunk_scores(c), acc_refs, m_old):
            m_new = jnp.maximum(m, jnp.max(s, axis=0, keepdims=True))
            acc_ref[...] = jnp.exp2(m - m_new) * acc_ref[...] + weighted_values(c, s, m_new)
            m_out.append(m_new)
        return tuple(m_out)

    n_trips, n_tail = divmod(n_chunks, chunks_per_trip)

    def fast_body(j, state):
        for u in range(chunks_per_trip):
            state = lazy_chunk(j * chunks_per_trip + u, state)
        return state

    for acc_ref in acc_refs:
        acc_ref[...] = jnp.zeros_like(acc_ref)
    start = (jnp.full((1, tq), MASK_VALUE, _F32), jnp.zeros((1, tq), _F32))
    state = lax.fori_loop(0, n_trips, fast_body, (start, start))
    for c in range(n_chunks - n_tail, n_chunks):
        state = lazy_chunk(c, state)
    worst_lag = jnp.max(jnp.maximum(jnp.maximum(state[0][1], -state[0][0]),
                                    jnp.maximum(state[1][1], -state[1][0])))

    @pl.when(jnp.logical_not(worst_lag <= DIFF_MAX_LAG))
    def _slow_path():
        lax.fori_loop(1, n_chunks, max_first_chunk, first_chunk())

    lam = (jnp.exp(jnp.sum(lq1_ref[...] * lk1_ref[...], axis=-1, keepdims=True))
           - jnp.exp(jnp.sum(lq2_ref[...] * lk2_ref[...], axis=-1, keepdims=True)) + LAM_INIT)
    o1 = acc1_ref[:DIFF_DV, :] / acc1_ref[DIFF_DV:DIFF_DV + 1, :]
    o2 = acc2_ref[:DIFF_DV, :] / acc2_ref[DIFF_DV:DIFF_DV + 1, :]
    o = o1 - lam * o2
    inv = lax.rsqrt(jnp.mean(o * o, axis=0, keepdims=True) + SUBLN_EPS)
    o = ((o * inv) * g_ref[...]) * (1.0 - LAM_INIT)
    o_ref[...] = o.T.astype(o_ref.dtype)


def _diff_attn(proj, lam_q1, lam_k1, lam_q2, lam_k2, g_subln, *, col_q, col_k, col_v, tq, tk):
    b, t, _ = proj.shape
    assert t % tq == 0 and t % tk == 0 and t // tk >= 2
    lam_spec = pl.BlockSpec((1, DIFF_DK), lambda bi, h, i: (0, 0))
    limit = _vmem_limit(2 * _nbytes((t, LANES), _BF16), 2 * _nbytes((tq, LANES), _BF16),
                        _nbytes((t, LANES), _BF16) // 2, _nbytes((DIFF_DV, tq), _F32),
                        2 * _nbytes((tk, tq), _F32))
    return pl.pallas_call(
        functools.partial(_diff_attn_kernel, tk=tk, chunks_per_trip=DIFF_CHUNKS_PER_TRIP),
        grid=(b, DIFF_HEADS, t // tq),
        in_specs=[
            pl.BlockSpec((None, tq, LANES), lambda bi, h, i: (bi, i, col_q + h)),
            pl.BlockSpec((None, t, LANES), lambda bi, h, i: (bi, 0, col_k + h)),
            pl.BlockSpec((None, t, LANES), lambda bi, h, i: (bi, 0, col_v + h)),
            lam_spec, lam_spec, lam_spec, lam_spec,
            pl.BlockSpec((DIFF_DV, 1), lambda bi, h, i: (0, 0)),
        ],
        out_specs=pl.BlockSpec((None, tq, DIFF_DV), lambda bi, h, i: (bi, i, h)),
        out_shape=jax.ShapeDtypeStruct((b, t, DIFF_HEADS * DIFF_DV), _BF16),
        scratch_shapes=[
            pltpu.VMEM((t // tk, DIFF_ACC_ROWS, tk), _BF16),
            pltpu.VMEM((DIFF_ACC_ROWS, tq), _F32),
            pltpu.VMEM((DIFF_ACC_ROWS, tq), _F32),
        ],
        compiler_params=pltpu.CompilerParams(
            dimension_semantics=("parallel", "parallel", "arbitrary"), vmem_limit_bytes=limit),
        name="diff_attn",
    )(proj, proj, proj, lam_q1.reshape(1, DIFF_DK), lam_k1.reshape(1, DIFF_DK),
      lam_q2.reshape(1, DIFF_DK), lam_k2.reshape(1, DIFF_DK), g_subln.reshape(DIFF_DV, 1))


def _out_proj_kernel(x_ref, na_ref, df_ref, w_ref, o_ref):
    n_na = na_ref.shape[1]
    acc = jnp.dot(na_ref[...], w_ref[:n_na, :], preferred_element_type=_F32)
    acc += jnp.dot(df_ref[...], w_ref[n_na:, :], preferred_element_type=_F32)
    o_ref[...] = x_ref[...] + acc


def _out_proj(x, na_o, df_o, w_out, *, tm, tn):
    m, d = x.shape
    k_na, k_df = na_o.shape[1], df_o.shape[1]
    assert m % tm == 0 and d % tn == 0 and w_out.shape == (k_na + k_df, d)
    limit = _vmem_limit(2 * _nbytes((tm, tn), _F32), _nbytes((tm, k_na + k_df), _BF16),
                        _nbytes((k_na + k_df, tn), _BF16))
    return pl.pallas_call(
        _out_proj_kernel,
        grid=(m // tm, d // tn),
        in_specs=[
            pl.BlockSpec((tm, tn), lambda i, j: (i, j)),
            pl.BlockSpec((tm, k_na), lambda i, j: (i, 0)),
            pl.BlockSpec((tm, k_df), lambda i, j: (i, 0)),
            pl.BlockSpec((k_na + k_df, tn), lambda i, j: (0, j), pipeline_mode=_resident(d == tn)),
        ],
        out_specs=pl.BlockSpec((tm, tn), lambda i, j: (i, j)),
        out_shape=jax.ShapeDtypeStruct((m, d), _F32),
        compiler_params=pltpu.CompilerParams(
            dimension_semantics=("parallel", "parallel"), vmem_limit_bytes=limit),
        name="out_proj",
    )(x, na_o, df_o, w_out)


def _mem_attn_kernel(h_ref, g_ref, wq_ref, kv_ref, w_ref, o_ref, q_ref, ctx_ref):
    d = h_ref.shape[1]
    hd = d // MEM_HEADS
    q_ref[...] = jnp.dot(_rms_normalize(h_ref, g_ref), wq_ref[...],
                         preferred_element_type=_F32).astype(q_ref.dtype)
    for h in range(MEM_HEADS):
        qh = q_ref[:, h * hd:(h + 1) * hd]
        kh = kv_ref[:, h * hd:(h + 1) * hd]
        vh = kv_ref[:, d + h * hd:d + (h + 1) * hd]
        s = lax.dot_general(qh, kh, _NT, preferred_element_type=_F32) * (hd ** -0.5)
        p = jnp.exp(s - jnp.max(s, axis=-1, keepdims=True))
        l = jnp.sum(p, axis=-1, keepdims=True)
        ctx = jnp.dot(p.astype(_BF16), vh, preferred_element_type=_F32) / l
        ctx_ref[:, h * hd:(h + 1) * hd] = ctx.astype(ctx_ref.dtype)
    o_ref[...] = h_ref[...] + jnp.dot(ctx_ref[...], w_ref[...], preferred_element_type=_F32)


def _mem_attn(h1, g_xattn, w_mq, kv, w_mo, *, seq_len, mem_tokens, tm):
    m, d = h1.shape
    assert m % tm == 0 and seq_len % tm == 0
    tiles_per_seq = seq_len // tm
    limit = _vmem_limit(_nbytes((mem_tokens, 2 * d), _BF16), 2 * _nbytes((tm, d), _F32),
                        _nbytes((d, d), _BF16), _nbytes((tm, d), _BF16))
    weight_spec = pl.BlockSpec((d, d), lambda i: (0, 0), pipeline_mode=_resident(True))
    return pl.pallas_call(
        _mem_attn_kernel,
        grid=(m // tm,),
        in_specs=[
            pl.BlockSpec((tm, d), lambda i: (i, 0)),
            pl.BlockSpec((1, d), lambda i: (0, 0)),
            weight_spec,
            pl.BlockSpec((mem_tokens, 2 * d), lambda i: (i // tiles_per_seq, 0)),
            weight_spec,
        ],
        out_specs=pl.BlockSpec((tm, d), lambda i: (i, 0)),
        out_shape=jax.ShapeDtypeStruct((m, d), _F32),
        scratch_shapes=[pltpu.VMEM((tm, d), _BF16), pltpu.VMEM((tm, d), _BF16)],
        compiler_params=pltpu.CompilerParams(
            dimension_semantics=("parallel",), vmem_limit_bytes=limit),
        name="mem_attn",
    )(h1, g_xattn.reshape(1, d), w_mq, kv, w_mo)


def _ffn_kernel(h_ref, g_ref, wg_ref, wu_ref, wd_ref, gf_ref, o_ref, hn_ref):
    f = pl.program_id(1)

    def down_projection(hn):
        gate = jnp.dot(hn, wg_ref[...], preferred_element_type=_F32)
        up = jnp.dot(hn, wu_ref[...], preferred_element_type=_F32)
        act = (gate * jax.nn.sigmoid(gate)) * up
        return jnp.dot(act.astype(_BF16), wd_ref[...], preferred_element_type=_F32)

    @pl.when(f == 0)
    def _first():
        x = h_ref[...]
        inv = lax.rsqrt(jnp.mean(x * x, axis=-1, keepdims=True) + RMS_EPS)
        hn = ((x * inv) * g_ref[...]).astype(hn_ref.dtype)
        hn_ref[...] = hn
        o_ref[...] = x + down_projection(hn)

    @pl.when(f > 0)
    def _accumulate():
        o_ref[...] += down_projection(hn_ref[...])

    @pl.when(f == pl.num_programs(1) - 1)
    def _finish():
        y = o_ref[...]
        inv = lax.rsqrt(jnp.mean(y * y, axis=-1, keepdims=True) + RMS_EPS)
        o_ref[...] = (y * inv) * gf_ref[...]


def _ffn(h2, g_ffn, w_gate_up, w_down, g_final, *, tm, tf):
    m, d = h2.shape
    d_ff = w_down.shape[0]
    assert m % tm == 0 and d_ff % tf == 0 and w_gate_up.shape == (d, 2 * d_ff)
    n_f = d_ff // tf
    limit = _vmem_limit(2 * _nbytes((tm, d), _F32), 3 * _nbytes((d, tf), _BF16),
                        _nbytes((tm, d), _BF16) // 2)
    return pl.pallas_call(
        _ffn_kernel,
        grid=(m // tm, n_f),
        in_specs=[
            pl.BlockSpec((tm, d), lambda i, f: (i, 0)),
            pl.BlockSpec((1, d), lambda i, f: (0, 0)),
            pl.BlockSpec((d, tf), lambda i, f: (0, f)),
            pl.BlockSpec((d, tf), lambda i, f: (0, n_f + f)),
            pl.BlockSpec((tf, d), lambda i, f: (f, 0)),
            pl.BlockSpec((1, d), lambda i, f: (0, 0)),
        ],
        out_specs=pl.BlockSpec((tm, d), lambda i, f: (i, 0)),
        out_shape=jax.ShapeDtypeStruct((m, d), _F32),
        scratch_shapes=[pltpu.VMEM((tm, d), _BF16)],
        compiler_params=pltpu.CompilerParams(
            dimension_semantics=("parallel", "arbitrary"), vmem_limit_bytes=limit),
        name="ffn",
    )(h2, g_ffn.reshape(1, d), w_gate_up, w_gate_up, w_down, g_final.reshape(1, d))


def _trunk(x, mem, p, na_bias):
    b, t, d = x.shape
    mem_tokens = mem.shape[1]
    m = b * t
    x2 = x.reshape(m, d)
    na_w = NA_HEADS * NA_HD
    qk_w = DIFF_HEADS * 2 * DIFF_DK
    q_cols = (3 * na_w, 3 * na_w + qk_w)
    k_cols = (3 * na_w + qk_w, 3 * na_w + 2 * qk_w)
    proj = _in_proj(x2, p["g_mix"], p["w_in"], _rope_tables(t), seq_len=t, tm=512,
                    q_cols=q_cols, k_cols=k_cols)
    proj = proj.reshape(b, t, -1)
    blk = lambda col: col // LANES
    na_o = _na_attn(proj, na_bias, col_q=0, col_k=blk(na_w), col_v=blk(2 * na_w))
    df_o = _diff_attn(proj, p["lam_q1"], p["lam_k1"], p["lam_q2"], p["lam_k2"], p["g_subln"],
                      col_q=blk(3 * na_w), col_k=blk(3 * na_w + qk_w),
                      col_v=blk(3 * na_w + 2 * qk_w), tq=1024, tk=512)
    h1 = _out_proj(x2, na_o.reshape(m, -1), df_o.reshape(m, -1), p["w_out"], tm=1024, tn=d)
    kv_mem = _norm_matmul(mem.reshape(b * mem_tokens, d), p["g_mem"], p["w_mkv"],
                          tm=min(512, b * mem_tokens), tn=512, name="mem_kv")
    h2 = _mem_attn(h1, p["g_xattn"], p["w_mq"], kv_mem, p["w_mo"], seq_len=t,
                   mem_tokens=mem_tokens, tm=512)
    y = _ffn(h2, p["g_ffn"], p["w_gate_up"], p["w_down"], p["g_final"], tm=1024, tf=512)
    return y.reshape(b, t, d)


def kernel(x_prompt, x_sample, mem_prompt, mem_sample, g_mix, w_in, rpb, lam_q1, lam_k1,
           lam_q2, lam_k2, g_subln, w_out, g_xattn, g_mem, w_mq, w_mkv, w_mo, g_ffn,
           w_gate_up, w_down, g_final):
    assert w_in.shape[0] == 1, "single-layer trunk"
    p = dict(
        g_mix=g_mix[0], w_in=w_in[0].astype(_BF16), lam_q1=lam_q1[0], lam_k1=lam_k1[0],
        lam_q2=lam_q2[0], lam_k2=lam_k2[0], g_subln=g_subln[0], w_out=w_out[0].astype(_BF16),
        g_xattn=g_xattn[0], g_mem=g_mem[0], w_mq=w_mq[0].astype(_BF16),
        w_mkv=w_mkv[0].astype(_BF16), w_mo=w_mo[0].astype(_BF16), g_ffn=g_ffn[0],
        w_gate_up=w_gate_up[0].astype(_BF16), w_down=w_down[0].astype(_BF16), g_final=g_final,
    )
    na_bias = _na_bias(rpb[0])
    return (_trunk(x_prompt, mem_prompt, p, na_bias), _trunk(x_sample, mem_sample, p, na_bias))
```
